```python
import jax, jax.numpy as jnp
from jax import lax
import numpy as np

D_MODEL = 1024
BATCH = 8
SEQ = 4096
DEPTH = 2

GRID_W = 64
CTX_LEN = 256
HEAD_DIM = 64
Q_BLOCK = 128
ROPE_THETA = 10000.0
EPS = 1e-6
M_INIT = -1e30
M_HEADS = 4
M_HEAD_DIM = 128
M_WIDTH = M_HEADS * M_HEAD_DIM
M_CHUNK = 128
CONV_K = 3
G_HEADS = 8
G_KV_HEADS = 2
W_HEADS = 8
W_KV_HEADS = 2
WINDOW = 128
N_HEADS = 8
NB_ROWS = 8
NB_COLS = 16
N_BRANCH = 4
BRANCH_WIDTH = 512
N_EXPERTS = 32
N_GROUPS = 4
EXPERTS_PER_GROUP = N_EXPERTS // N_GROUPS
TOP_K = 2
D_EXPERT = 512
MOE_BLOCK = 128
PROJ_WIDTHS = (M_WIDTH, M_WIDTH, M_WIDTH, M_WIDTH, 4 * M_HEADS,
               G_HEADS * HEAD_DIM, G_KV_HEADS * HEAD_DIM, G_KV_HEADS * HEAD_DIM,
               W_HEADS * HEAD_DIM, W_KV_HEADS * HEAD_DIM, W_KV_HEADS * HEAD_DIM,
               N_HEADS * HEAD_DIM, N_HEADS * HEAD_DIM, N_HEADS * HEAD_DIM,
               N_BRANCH * D_MODEL)
N_PROJ = sum(PROJ_WIDTHS)

kernel_name = 'hybrid_mlstm_gqa_swa_natten_moe_dit_block'


def rms_norm(x, g):
    xf = x.astype(jnp.float32)
    y = xf * lax.rsqrt(jnp.mean(xf * xf, axis=-1, keepdims=True) + EPS)
    return y.astype(x.dtype) * g


def split_projection(p):
    idx, acc = [], 0
    for w in PROJ_WIDTHS[:-1]:
        acc += w
        idx.append(acc)
    return jnp.split(p, idx, axis=-1)


def to_heads(a, n_heads):
    return a.reshape(a.shape[0], a.shape[1], n_heads, HEAD_DIM)


def axial_rope_tables(n_tokens):
    t = jnp.arange(n_tokens, dtype=jnp.int32)
    row = (t // GRID_W).astype(jnp.float32)
    col = (t % GRID_W).astype(jnp.float32)
    n_freq = HEAD_DIM // 4
    inv_freq = ROPE_THETA ** (-jnp.arange(n_freq, dtype=jnp.float32) / n_freq)
    ang = jnp.concatenate([row[:, None] * inv_freq, col[:, None] * inv_freq], axis=-1)
    return jnp.cos(ang), jnp.sin(ang)


def apply_rope(x, cos, sin):
    shp = x.shape
    xf = x.astype(jnp.float32).reshape(shp[:-1] + (shp[-1] // 2, 2))
    x0, x1 = xf[..., 0], xf[..., 1]
    c = cos[None, :, None, :]
    s = sin[None, :, None, :]
    out = jnp.stack([x0 * c - x1 * s, x0 * s + x1 * c], axis=-1)
    return out.reshape(shp).astype(x.dtype)


def ctx_attention(q, k, v, sink=None):
    B, C, Hq, d = q.shape
    Hkv = k.shape[2]
    R = Hq // Hkv
    qg = q.reshape(B, C, Hkv, R, d)
    s = jnp.einsum('bqgrd,bkgd->bgrqk', qg, k).astype(jnp.float32) * d ** -0.5
    if sink is not None:
        sink_col = jnp.broadcast_to(sink.astype(jnp.float32).reshape(1, Hkv, R, 1, 1), (B, Hkv, R, C, 1))
        s = jnp.concatenate([s, sink_col], axis=-1)
    p = jax.nn.softmax(s, axis=-1)[..., :C].astype(v.dtype)
    return jnp.einsum('bgrqk,bkgd->bqgrd', p, v).reshape(B, C, Hq, d)


def global_attention(q, k_all, v_all):
    B, S, Hq, d = q.shape
    Hkv = k_all.shape[2]
    R = Hq // Hkv
    nb = S // Q_BLOCK
    qb = jnp.moveaxis(q.reshape(B, nb, Q_BLOCK, Hkv, R, d), 1, 0)

    def block(qi):
        s = jnp.einsum('bqgrd,bkgd->bgrqk', qi, k_all).astype(jnp.float32) * d ** -0.5
        p = jax.nn.softmax(s, axis=-1).astype(v_all.dtype)
        return jnp.einsum('bgrqk,bkgd->bqgrd', p, v_all)

    o = lax.map(block, qb)
    return jnp.moveaxis(o, 0, 1).reshape(B, S, Hq, d)


def window_attention(q, k, v, kc, vc, sink):
    B, S, Hq, d = q.shape
    Hkv = k.shape[2]
    R = Hq // Hkv
    C = kc.shape[1]
    nb = S // Q_BLOCK
    span = Q_BLOCK + 2 * WINDOW
    pad = ((0, 0), (WINDOW, WINDOW), (0, 0), (0, 0))
    kp = jnp.pad(k, pad)
    vp = jnp.pad(v, pad)
    qb = jnp.moveaxis(q.reshape(B, nb, Q_BLOCK, Hkv, R, d), 1, 0)
    sink_col = jnp.broadcast_to(sink.astype(jnp.float32).reshape(1, Hkv, R, 1, 1), (B, Hkv, R, Q_BLOCK, 1))
    scale = d ** -0.5

    def block(args):
        qi, i = args
        start = i * Q_BLOCK
        ks = lax.dynamic_slice_in_dim(kp, start, span, axis=1)
        vs = lax.dynamic_slice_in_dim(vp, start, span, axis=1)
        qpos = start + jnp.arange(Q_BLOCK)
        kpos = start - WINDOW + jnp.arange(span)
        valid = (jnp.abs(kpos[None, :] - qpos[:, None]) <= WINDOW) & (kpos >= 0)[None, :] & (kpos < S)[None, :]
        s_loc = jnp.einsum('bqgrd,bkgd->bgrqk', qi, ks).astype(jnp.float32) * scale
        s_loc = jnp.where(valid, s_loc, -jnp.inf)
        s_ctx = jnp.einsum('bqgrd,bcgd->bgrqc', qi, kc).astype(jnp.float32) * scale
        p = jax.nn.softmax(jnp.concatenate([s_loc, s_ctx, sink_col], axis=-1), axis=-1).astype(v.dtype)
        return (jnp.einsum('bgrqk,bkgd->bqgrd', p[..., :span], vs)
                + jnp.einsum('bgrqc,bcgd->bqgrd', p[..., span:span + C], vc))

    o = lax.map(block, (qb, jnp.arange(nb)))
    return jnp.moveaxis(o, 0, 1).reshape(B, S, Hq, d)


def neighbourhood_attention(q, k, v, kc, vc, rel_bias):
    B, S, H, d = q.shape
    rows = S // GRID_W
    kh = min(NB_ROWS, rows)
    scale = d ** -0.5
    qg = jnp.moveaxis(q.reshape(B, rows, GRID_W, H, d), 1, 0)
    kg = k.reshape(B, rows, GRID_W, H, d)
    vg = v.reshape(B, rows, GRID_W, H, d)
    col = np.arange(GRID_W)
    col_start = np.clip(col - NB_COLS // 2, 0, GRID_W - NB_COLS)
    col_idx = col_start[:, None] + np.arange(NB_COLS)[None, :]
    dcol = col_idx - col[:, None]
    bias_cols = rel_bias[:, :, dcol + NB_COLS - 1]
    n_loc = kh * NB_COLS

    def block(args):
        qr, r = args
        r0 = jnp.clip(r - kh // 2, 0, rows - kh)
        ks = lax.dynamic_slice_in_dim(kg, r0, kh, axis=1)[:, :, col_idx]
        vs = lax.dynamic_slice_in_dim(vg, r0, kh, axis=1)[:, :, col_idx]
        drow = r0 + jnp.arange(kh) - r
        bias = jnp.moveaxis(bias_cols[:, drow + NB_ROWS - 1], 1, 2)
        s_loc = jnp.einsum('bqhd,bkqjhd->bhqkj', qr, ks).astype(jnp.float32) * scale + bias.astype(jnp.float32)
        s_loc = s_loc.reshape(B, H, GRID_W, n_loc)
        s_ctx = jnp.einsum('bqhd,bchd->bhqc', qr, kc).astype(jnp.float32) * scale
        p = jax.nn.softmax(jnp.concatenate([s_loc, s_ctx], axis=-1), axis=-1).astype(v.dtype)
        p_loc = p[..., :n_loc].reshape(B, H, GRID_W, kh, NB_COLS)
        return (jnp.einsum('bhqkj,bkqjhd->bqhd', p_loc, vs)
                + jnp.einsum('bhqc,bchd->bqhd', p[..., n_loc:], vc))

    o = lax.map(block, (qg, jnp.arange(rows)))
    return jnp.moveaxis(o, 0, 1).reshape(B, S, H, d)


def short_conv(x, w, b):
    y = lax.conv_general_dilated(x, w[:, None, :].astype(x.dtype), window_strides=(1,),
                                 padding=[(CONV_K // 2, CONV_K // 2)],
                                 dimension_numbers=('NWC', 'WIO', 'NWC'),
                                 feature_group_count=x.shape[-1])
    return y + b


def mlstm_inputs(q_raw, k_raw, v_raw, if_raw, conv_w, conv_b, gate_b):
    qk = jax.nn.silu(short_conv(jnp.concatenate([q_raw, k_raw], axis=-1), conv_w, conv_b))
    q, k = jnp.split(qk, 2, axis=-1)
    B, T, _ = q.shape

    def heads(a):
        return jnp.swapaxes(a.reshape(B, T, M_HEADS, M_HEAD_DIM), 1, 2).astype(jnp.float32)

    g = (if_raw.reshape(B, T, 2, 2, M_HEADS) + gate_b).astype(jnp.float32)
    g = jnp.transpose(g, (2, 3, 0, 4, 1))
    log_i = g[:, 0]
    log_f = jax.nn.log_sigmoid(g[:, 1])
    return heads(q), heads(k) * M_HEAD_DIM ** -0.5, heads(v_raw), log_i, log_f


def mlstm_chunk_scan(q, k, v, log_i, log_f, state):
    B, H, T, _ = q.shape
    dv = v.shape[-1]
    L = M_CHUNK
    nc = T // L

    def chunks(a):
        return jnp.moveaxis(a.reshape((B, H, nc, L) + a.shape[3:]), 2, 0)

    tril = jnp.tril(jnp.ones((L, L), dtype=bool))

    def step(carry, xs):
        C, n, m = carry
        qc, kc, vc, ic, fc = xs
        b = jnp.cumsum(fc, axis=-1)
        dmat = jnp.where(tril, b[..., :, None] - b[..., None, :] + ic[..., None, :], -jnp.inf)
        m_inter = b + m[..., None]
        m_t = jnp.maximum(dmat.max(axis=-1), m_inter)
        a = jnp.einsum('bhtd,bhsd->bhts', qc, kc) * jnp.exp(dmat - m_t[..., None])
        inter = jnp.exp(m_inter - m_t)
        num = jnp.einsum('bhts,bhsv->bhtv', a, vc) + inter[..., None] * jnp.einsum('bhtd,bhdv->bhtv', qc, C)
        den = a.sum(axis=-1) + inter * jnp.einsum('bhtd,bhd->bht', qc, n)
        h = num / jnp.maximum(jnp.abs(den), jnp.exp(-m_t))[..., None]
        b_end = b[..., -1]
        g = b_end[..., None] - b + ic
        m_new = jnp.maximum(b_end + m, g.max(axis=-1))
        decay = jnp.exp(b_end + m - m_new)
        wg = jnp.exp(g - m_new[..., None])
        C_new = decay[..., None, None] * C + jnp.einsum('bhs,bhsd,bhsv->bhdv', wg, kc, vc)
        n_new = decay[..., None] * n + jnp.einsum('bhs,bhsd->bhd', wg, kc)
        return (C_new, n_new, m_new), h

    state, hs = lax.scan(step, state, (chunks(q), chunks(k), chunks(v), chunks(log_i), chunks(log_f)))
    return jnp.moveaxis(hs, 0, 2).reshape(B, H, T, dv), state


def mlstm_bidirectional(q, k, v, li, lf, qc, kc, vc, lic, lfc):
    B, H, _, dk = q.shape
    dv = v.shape[-1]
    state0 = (jnp.zeros((B, H, dk, dv), jnp.float32), jnp.zeros((B, H, dk), jnp.float32),
              jnp.full((B, H), M_INIT, jnp.float32))
    outs_l, outs_c = [], []
    for d in range(2):
        rev = (lambda a: jnp.flip(a, axis=2)) if d == 1 else (lambda a: a)
        hc_d, st = mlstm_chunk_scan(rev(qc), rev(kc), rev(vc), rev(lic[d]), rev(lfc[d]), state0)
        hl_d, _ = mlstm_chunk_scan(rev(q), rev(k), rev(v), rev(li[d]), rev(lf[d]), st)
        outs_l.append(rev(hl_d))
        outs_c.append(rev(hc_d))
    return outs_l[0] + outs_l[1], outs_c[0] + outs_c[1]


def mlstm_output(h, o_raw, norm_g):
    h = jnp.swapaxes(h, 1, 2)
    h = h * lax.rsqrt(jnp.mean(h * h, axis=-1, keepdims=True) + EPS)
    B, T = h.shape[:2]
    return (h.reshape(B, T, M_WIDTH).astype(o_raw.dtype) * norm_g) * jax.nn.sigmoid(o_raw)


def merge_branches(branches, gate_raw, w_branch, w_out):
    B, T, _ = gate_raw.shape
    g = jax.nn.sigmoid(gate_raw.reshape(B, T, N_BRANCH, D_MODEL))
    y = g[:, :, 0] * (branches[0] @ w_branch[0])
    for i in range(1, N_BRANCH):
        y = y + g[:, :, i] * (branches[i] @ w_branch[i])
    return y @ w_out


def token_mixer(h, hc, cos, sin, w_in, m_conv_w, m_conv_b, m_gate_b, m_norm_g, g_qnorm, g_knorm,
                w_sink, n_rel_bias, w_branch, w_out, need_ctx):
    B, S, _ = h.shape
    C = hc.shape[1]
    (mq, mk, mv, mo, mif, gq, gk, gv, wq, wk, wv, nq, nk, nv, gate) = split_projection(h @ w_in)
    (mqc, mkc, mvc, moc, mifc, gqc, gkc, gvc, wqc, wkc, wvc, nqc, nkc, nvc, gatec) = split_projection(hc @ w_in)

    a_q, a_k, a_v, a_li, a_lf = mlstm_inputs(mq, mk, mv, mif, m_conv_w, m_conv_b, m_gate_b)
    ac_q, ac_k, ac_v, ac_li, ac_lf = mlstm_inputs(mqc, mkc, mvc, mifc, m_conv_w, m_conv_b, m_gate_b)
    h_lat, h_ctx = mlstm_bidirectional(a_q, a_k, a_v, a_li, a_lf, ac_q, ac_k, ac_v, ac_li, ac_lf)
    out_a = mlstm_output(h_lat, mo, m_norm_g)

    b_q = apply_rope(rms_norm(to_heads(gq, G_HEADS), g_qnorm), cos, sin)
    b_k = apply_rope(rms_norm(to_heads(gk, G_KV_HEADS), g_knorm), cos, sin)
    b_v = to_heads(gv, G_KV_HEADS)
    bc_q = rms_norm(to_heads(gqc, G_HEADS), g_qnorm)
    bc_k = rms_norm(to_heads(gkc, G_KV_HEADS), g_knorm)
    bc_v = to_heads(gvc, G_KV_HEADS)
    out_b = global_attention(b_q, jnp.concatenate([b_k, bc_k], axis=1),
                             jnp.concatenate([b_v, bc_v], axis=1)).reshape(B, S, -1)

    c_q = apply_rope(to_heads(wq, W_HEADS), cos, sin)
    c_k = apply_rope(to_heads(wk, W_KV_HEADS), cos, sin)
    c_v = to_heads(wv, W_KV_HEADS)
    cc_q, cc_k, cc_v = to_heads(wqc, W_HEADS), to_heads(wkc, W_KV_HEADS), to_heads(wvc, W_KV_HEADS)
    out_c = window_attention(c_q, c_k, c_v, cc_k, cc_v, w_sink).reshape(B, S, -1)

    d_q, d_k, d_v = to_heads(nq, N_HEADS), to_heads(nk, N_HEADS), to_heads(nv, N_HEADS)
    dc_q, dc_k, dc_v = to_heads(nqc, N_HEADS), to_heads(nkc, N_HEADS), to_heads(nvc, N_HEADS)
    out_d = neighbourhood_attention(d_q, d_k, d_v, dc_k, dc_v, n_rel_bias).reshape(B, S, -1)

    y = merge_branches([out_a, out_b, out_c, out_d], gate, w_branch, w_out)
    yc = None
    if need_ctx:
        oc_a = mlstm_output(h_ctx, moc, m_norm_g)
        oc_b = ctx_attention(bc_q, bc_k, bc_v).reshape(B, C, -1)
        oc_c = ctx_attention(cc_q, cc_k, cc_v, w_sink).reshape(B, C, -1)
        oc_d = ctx_attention(dc_q, dc_k, dc_v).reshape(B, C, -1)
        yc = merge_branches([oc_a, oc_b, oc_c, oc_d], gatec, w_branch, w_out)
    return y, yc


def moe_ffn(t, w_router, router_bias, e_gate, e_up, e_down):
    T, D = t.shape
    scores = jax.nn.sigmoid((t @ w_router).astype(jnp.float32))
    sel = (scores + router_bias.astype(jnp.float32)).reshape(T, N_GROUPS, EXPERTS_PER_GROUP)
    group_score = lax.top_k(sel, TOP_K)[0].sum(axis=-1)
    grp = jnp.argmax(group_score, axis=-1)
    in_group = jnp.take_along_axis(sel, grp[:, None, None], axis=1)[:, 0]
    _, local = lax.top_k(in_group, TOP_K)
    expert = grp[:, None] * EXPERTS_PER_GROUP + local
    w = jnp.take_along_axis(scores, expert, axis=1)
    w = w / w.sum(axis=-1, keepdims=True)

    n_assign = T * TOP_K
    e_flat = expert.reshape(-1)
    tok = jnp.repeat(jnp.arange(T, dtype=jnp.int32), TOP_K)
    w_flat = w.reshape(-1)
    order = jnp.argsort(e_flat)
    e_s, tok_s, w_s = e_flat[order], tok[order], w_flat[order]
    counts = jnp.bincount(e_flat, length=N_EXPERTS)
    starts = jnp.cumsum(counts) - counts
    padded = ((counts + MOE_BLOCK - 1) // MOE_BLOCK) * MOE_BLOCK
    pends = jnp.cumsum(padded)
    pstarts = pends - padded
    dest = pstarts[e_s] + (jnp.arange(n_assign) - starts[e_s])
    nblk = -(-(n_assign + N_EXPERTS * (MOE_BLOCK - 1)) // MOE_BLOCK)
    cap = nblk * MOE_BLOCK
    slot_tok = jnp.full((cap,), T, dtype=jnp.int32).at[dest].set(tok_s)
    slot_w = jnp.zeros((cap,), jnp.float32).at[dest].set(w_s)
    blk_e = jnp.minimum(jnp.searchsorted(pends, jnp.arange(nblk) * MOE_BLOCK, side='right'), N_EXPERTS - 1)
    t_pad = jnp.concatenate([t, jnp.zeros((1, D), t.dtype)], axis=0)
    xs = t_pad[slot_tok].reshape(nblk, MOE_BLOCK, D)

    def expert_block(args):
        xb, e = args
        hmid = jax.nn.silu(xb @ e_gate[e]) * (xb @ e_up[e])
        return hmid @ e_down[e]

    ys = lax.map(expert_block, (xs, blk_e)).reshape(cap, D)
    out = jnp.zeros((T + 1, D), ys.dtype).at[slot_tok].add(ys * slot_w[:, None].astype(ys.dtype))
    return out[:T]


def setup_inputs(seed: int = 0) -> dict:
    key = jax.random.key(seed)
    ks = jax.random.split(key, 32)
    f32 = jnp.float32
    L, D = DEPTH, D_MODEL

    def nrm(k, shape, scale):
        return jax.random.normal(k, shape, f32) * scale

    gate_base = jnp.array([-1.0, 3.0], f32).reshape(1, 1, 2, 1)
    return {
        'x': nrm(ks[0], (BATCH, SEQ, D), 1.0),
        'c': nrm(ks[1], (BATCH, D), 1.0),
        'ctx': nrm(ks[2], (BATCH, CTX_LEN, D), 1.0),
        'c_ctx': nrm(ks[3], (D,), 1.0),
        'w_mod': nrm(ks[4], (L, D, 6 * D), 0.5 * D ** -0.5),
        'b_mod': nrm(ks[5], (L, 6 * D), 0.02),
        'norm1_g': 1.0 + nrm(ks[6], (L, D), 0.05),
        'norm2_g': 1.0 + nrm(ks[7], (L, D), 0.05),
        'w_in': nrm(ks[8], (L, D, N_PROJ), D ** -0.5),
        'm_conv_w': nrm(ks[9], (L, CONV_K, 2 * M_WIDTH), CONV_K ** -0.5),
        'm_conv_b': nrm(ks[10], (L, 2 * M_WIDTH), 0.02),
        'm_gate_b': gate_base + nrm(ks[11], (L, 2, 2, M_HEADS), 0.2),
        'm_norm_g': 1.0 + nrm(ks[12], (L, M_WIDTH), 0.05),
        'g_qnorm': 1.0 + nrm(ks[13], (L, HEAD_DIM), 0.05),
        'g_knorm': 1.0 + nrm(ks[14], (L, HEAD_DIM), 0.05),
        'w_sink': nrm(ks[15], (L, W_HEADS), 0.5),
        'n_rel_bias': nrm(ks[16], (L, N_HEADS, 2 * NB_ROWS - 1, 2 * NB_COLS - 1), 0.5),
        'w_branch': nrm(ks[17], (L, N_BRANCH, BRANCH_WIDTH, D), BRANCH_WIDTH ** -0.5),
        'w_out': nrm(ks[18], (L, D, D), D ** -0.5),
        'w_router': nrm(ks[19], (D, N_EXPERTS), D ** -0.5),
        'router_bias': nrm(ks[20], (N_EXPERTS,), 0.01),
        'e_gate': nrm(ks[21], (L, N_EXPERTS, D, D_EXPERT), D ** -0.5),
        'e_up': nrm(ks[22], (L, N_EXPERTS, D, D_EXPERT), D ** -0.5),
        'e_down': nrm(ks[23], (L, N_EXPERTS, D_EXPERT, D), D_EXPERT ** -0.5),
        'final_g': 1.0 + nrm(ks[24], (D,), 0.05),
    }


def reference(x, c, ctx, c_ctx, w_mod, b_mod, norm1_g, norm2_g, w_in, m_conv_w, m_conv_b, m_gate_b,
              m_norm_g, g_qnorm, g_knorm, w_sink, n_rel_bias, w_branch, w_out, w_router, router_bias,
              e_gate, e_up, e_down, final_g):
    B, S, D = x.shape
    cos, sin = axial_rope_tables(S)
    xc = ctx
    for l in range(DEPTH):
        need_ctx = l < DEPTH - 1
        mod = jax.nn.silu(c) @ w_mod[l] + b_mod[l]
        sh1, sc1, ga1, sh2, sc2, ga2 = jnp.split(mod[:, None, :], 6, axis=-1)
        modc = jax.nn.silu(c_ctx) @ w_mod[l] + b_mod[l]
        csh1, csc1, cga1, csh2, csc2, cga2 = jnp.split(modc, 6)

        h = rms_norm(x, norm1_g[l]) * (1 + sc1) + sh1
        hc = rms_norm(xc, norm1_g[l]) * (1 + csc1) + csh1
        y, yc = token_mixer(h, hc, cos, sin, w_in[l], m_conv_w[l], m_conv_b[l], m_gate_b[l], m_norm_g[l],
                            g_qnorm[l], g_knorm[l], w_sink[l], n_rel_bias[l], w_branch[l], w_out[l], need_ctx)
        x = x + ga1 * y
        h2 = rms_norm(x, norm2_g[l]) * (1 + sc2) + sh2
        tokens = h2.reshape(B * S, D)
        if need_ctx:
            xc = xc + cga1 * yc
            hc2 = rms_norm(xc, norm2_g[l]) * (1 + csc2) + csh2
            tokens = jnp.concatenate([tokens, hc2.reshape(-1, D)], axis=0)
        f = moe_ffn(tokens, w_router, router_bias, e_gate[l], e_up[l], e_down[l])
        x = x + ga2 * f[:B * S].reshape(B, S, D)
        if need_ctx:
            xc = xc + cga2 * f[B * S:].reshape(B, -1, D)
    return rms_norm(x, final_g)
```

```python
import functools

import numpy as np
import jax
import jax.numpy as jnp
from jax import lax
from jax.experimental import pallas as pl
from jax.experimental.pallas import tpu as pltpu

F32 = jnp.float32
BF16 = jnp.bfloat16
HIGHEST = lax.Precision.HIGHEST

D_MODEL = 1024
DEPTH = 2
GRID_W = 64
HEAD_DIM = 64
ROPE_THETA = 10000.0
EPS = 1e-6
M_INIT = -1e30
NEG = -1e30
M_HEADS = 4
M_HEAD_DIM = 128
M_WIDTH = M_HEADS * M_HEAD_DIM
M_CHUNK = 128
N_QHEADS = 8
WINDOW = 128
NB_ROWS = 8
NB_COLS = 16
N_BRANCH = 4
BRANCH_WIDTH = 512
N_EXPERTS = 32
N_GROUPS = 4
EXPERTS_PER_GROUP = N_EXPERTS // N_GROUPS
D_EXPERT = 512
MOE_BLOCK = 256
LANES = 128
VMEM_LIMIT = 56 * 1024 * 1024

COL_MQ, COL_MK, COL_MV, COL_MO = 0, 512, 1024, 1536
COL_GQ, COL_WQ, COL_NQ, COL_NK, COL_NV = 2048, 2560, 3072, 3584, 4096
COL_GK, COL_GV, COL_WK, COL_WV = 4608, 4736, 4864, 4992
COL_GATE = 5120
N_PROJ_OUT = COL_GATE + N_BRANCH * D_MODEL
PAIR_ORDER = (0, 4, 1, 5, 2, 6, 3, 7)


def _cparams(sem):
    return pltpu.CompilerParams(dimension_semantics=sem, vmem_limit_bytes=VMEM_LIMIT)


def _sigmoid(x):
    return 1.0 / (1.0 + jnp.exp(-x))


def _lo_mask(shape):
    return lax.broadcasted_iota(jnp.int32, shape, len(shape) - 1) < HEAD_DIM


def _mod_kernel(c_ref, w_ref, b_ref, o_ref):
    c = c_ref[...]
    a = c * _sigmoid(c)
    o_ref[0] = jnp.dot(a, w_ref[0], precision=HIGHEST, preferred_element_type=F32) + b_ref[0]


def modulation(c_all, w_mod, b_mod):
    L, D, N = w_mod.shape
    R = c_all.shape[0]
    tn = 1536
    return pl.pallas_call(
        _mod_kernel,
        grid=(L, N // tn),
        in_specs=[pl.BlockSpec((R, D), lambda l, j: (0, 0)),
                  pl.BlockSpec((1, D, tn), lambda l, j: (l, 0, j)),
                  pl.BlockSpec((1, 1, tn), lambda l, j: (l, 0, j))],
        out_specs=pl.BlockSpec((1, R, tn), lambda l, j: (l, 0, j)),
        out_shape=jax.ShapeDtypeStruct((L, R, N), F32),
        compiler_params=_cparams(("arbitrary", "arbitrary")),
        name="modulation",
    )(c_all, w_mod, b_mod.reshape(L, 1, N))


def _log_sigmoid(z):
    return jnp.minimum(z, 0.0) - jnp.log(1.0 + jnp.exp(-jnp.abs(z)))


def _inproj_kernel(*refs, has_f):
    if has_f:
        (x_ref, f_ref, ga_ref, g_ref, sc_ref, sh_ref, w_ref, wif_ref, gb_ref,
         xo_ref, p_ref, gate_ref, h_scr) = refs
    else:
        x_ref, g_ref, sc_ref, sh_ref, w_ref, wif_ref, gb_ref, p_ref, gate_ref, h_scr = refs

    @pl.when(pl.program_id(1) == 0)
    def _():
        x = x_ref[...]
        if has_f:
            x = x + ga_ref[0] * f_ref[...]
            xo_ref[...] = x
        y = x * lax.rsqrt(jnp.mean(x * x, axis=-1, keepdims=True) + EPS)
        h = (y * g_ref[...]) * (1.0 + sc_ref[0]) + sh_ref[0]
        hb = h.astype(BF16)
        h_scr[...] = hb
        z = jnp.dot(hb, wif_ref[...], preferred_element_type=F32) + gb_ref[...]
        lane = lax.broadcasted_iota(jnp.int32, z.shape, 1)
        is_forget = (lane & 0xF4) == 4
        gate_ref[...] = jnp.where(is_forget, _log_sigmoid(z), z)

    p_ref[...] = jnp.dot(h_scr[...], w_ref[...], preferred_element_type=F32).astype(BF16)


def in_projection(x, g, sc, sh, w, wif, gb, rows_per_mod, f=None, ga=None):
    R, D = x.shape
    N = w.shape[1]
    tm = min(1024, R)
    tn = 1024
    tpm = rows_per_mod // tm
    has_f = f is not None
    row = lambda i, j: (i, 0)
    mod = lambda i, j: (i // tpm, 0, 0)
    fixed = lambda i, j: (0, 0)
    in_specs = [pl.BlockSpec((tm, D), row)]
    args = [x]
    if has_f:
        in_specs += [pl.BlockSpec((tm, D), row), pl.BlockSpec((1, 1, D), mod)]
        args += [f, ga]
    in_specs += [pl.BlockSpec((1, D), fixed), pl.BlockSpec((1, 1, D), mod), pl.BlockSpec((1, 1, D), mod),
                 pl.BlockSpec((D, tn), lambda i, j: (0, j)), pl.BlockSpec((D, LANES), fixed),
                 pl.BlockSpec((1, LANES), fixed)]
    args += [g.reshape(1, D), sc, sh, w, wif, gb]
    out_specs = [pl.BlockSpec((tm, tn), lambda i, j: (i, j)), pl.BlockSpec((tm, LANES), row)]
    out_shape = [jax.ShapeDtypeStruct((R, N), BF16), jax.ShapeDtypeStruct((R, LANES), F32)]
    if has_f:
        out_specs = [pl.BlockSpec((tm, D), row)] + out_specs
        out_shape = [jax.ShapeDtypeStruct((R, D), F32)] + out_shape
    return pl.pallas_call(
        functools.partial(_inproj_kernel, has_f=has_f),
        grid=(R // tm, N // tn),
        in_specs=in_specs, out_specs=out_specs, out_shape=out_shape,
        scratch_shapes=[pltpu.VMEM((tm, D), BF16)],
        compiler_params=_cparams(("arbitrary", "arbitrary")),
        name="in_projection",
    )(*args)


def _mconv_kernel(x_ref, w_ref, b_ref, o_ref, *, k_scale):
    x = x_ref[0].astype(F32)
    T = x.shape[0]
    row = lax.broadcasted_iota(jnp.int32, x.shape, 0)
    x_prev = jnp.where(row == 0, 0.0, pltpu.roll(x, 1, 0))
    x_next = jnp.where(row == T - 1, 0.0, pltpu.roll(x, T - 1, 0))
    y = x_prev * w_ref[0:1, :] + x * w_ref[1:2, :] + x_next * w_ref[2:3, :] + b_ref[...]
    y = y * _sigmoid(y)
    scale = jnp.where(pl.program_id(1) >= pl.num_programs(1) // 2, k_scale, 1.0)
    o_ref[0] = (y * scale).astype(BF16)


def mlstm_conv(p3, conv_w, conv_b):
    B, T, _ = p3.shape
    tc = 256
    nct = 2 * M_WIDTH // tc
    return pl.pallas_call(
        functools.partial(_mconv_kernel, k_scale=M_HEAD_DIM ** -0.5),
        grid=(B, nct),
        in_specs=[pl.BlockSpec((1, T, tc), lambda b, j: (b, 0, j)),
                  pl.BlockSpec((3, tc), lambda b, j: (0, j)),
                  pl.BlockSpec((1, tc), lambda b, j: (0, j))],
        out_specs=pl.BlockSpec((1, T, tc), lambda b, j: (b, 0, j)),
        out_shape=jax.ShapeDtypeStruct((B, T, 2 * M_WIDTH), BF16),
        compiler_params=_cparams(("arbitrary", "arbitrary")),
        name="mlstm_conv",
    )(p3, conv_w, conv_b.reshape(1, -1))


def _mlstm_kernel(q0_ref, k0_ref, v0_ref, g0_ref, q1_ref, k1_ref, v1_ref, g1_ref, cin_ref, min_ref,
                  h0_ref, h1_ref, cout_ref, mout_ref, c_scr, m_scr):
    c = pl.program_id(1)
    L = M_CHUNK

    @pl.when(c == 0)
    def _():
        c_scr[...] = cin_ref[0]
        m_scr[...] = min_ref[0]

    row = lax.broadcasted_iota(jnp.int32, (L, L), 0)
    col = lax.broadcasted_iota(jnp.int32, (L, L), 1)
    ones_col = jnp.where(col == 0, 1.0, 0.0).astype(BF16)
    all_ones = jnp.ones((L, L), F32)
    dirs = ((q0_ref, k0_ref, v0_ref, g0_ref, h0_ref), (q1_ref, k1_ref, v1_ref, g1_ref, h1_ref))
    for d, (q_ref, k_ref, v_ref, g_ref, h_ref) in enumerate(dirs):
        G = g_ref[0]
        tri = jnp.where(col <= row, 1.0, 0.0) if d == 0 else jnp.where(col >= row, 1.0, 0.0)
        CS = jnp.dot(tri, G, precision=HIGHEST, preferred_element_type=F32)
        TOT = jnp.dot(all_ones, G, precision=HIGHEST, preferred_element_type=F32)
        GT, CST, TOTT = G.T, CS.T, TOT.T
        keep = (col <= row) if d == 0 else (col >= row)
        for h in range(M_HEADS):
            ci, cf, sidx = d * 8 + h, d * 8 + 4 + h, d * M_HEADS + h
            bc, br = CS[:, cf:cf + 1], CST[cf:cf + 1, :]
            ic, ir = G[:, ci:ci + 1], GT[ci:ci + 1, :]
            bend_c, bend_r = TOT[:, cf:cf + 1], TOTT[cf:cf + 1, :]
            m_prev = m_scr[sidx:sidx + 1, :]
            m_prev1 = m_prev[:, 0:1]
            hs = slice(h * M_HEAD_DIM, (h + 1) * M_HEAD_DIM)
            q = q_ref[0, :, hs]
            k = k_ref[0, :, hs]
            v_aug = jnp.concatenate([v_ref[0, :, hs], ones_col], axis=1)
            c_prev = c_scr[sidx]

            dmat = jnp.where(keep, bc - br + ir, NEG)
            m_inter = bc + m_prev1
            m_t = jnp.maximum(jnp.max(dmat, axis=1, keepdims=True), m_inter)
            s_qk = lax.dot_general(q, k, (((1,), (1,)), ((), ())), preferred_element_type=F32)
            a = s_qk * jnp.exp(dmat - m_t)
            inter = jnp.exp(m_inter - m_t)
            nd = (jnp.dot(a.astype(BF16), v_aug, preferred_element_type=F32)
                  + inter * jnp.dot(q, c_prev.astype(BF16), preferred_element_type=F32))
            den = nd[:, M_HEAD_DIM:M_HEAD_DIM + 1]
            hval = nd[:, :M_HEAD_DIM] / jnp.maximum(jnp.abs(den), jnp.exp(-m_t))
            h_ref[0, :, hs] = hval.astype(BF16)

            g_row = bend_r - br + ir
            m_new = jnp.maximum(bend_r + m_prev, jnp.max(g_row, axis=1, keepdims=True))
            m_new1 = m_new[:, 0:1]
            decay = jnp.exp(bend_r[:, 0:1] + m_prev1 - m_new1)
            wg_col = jnp.exp(bend_c - bc + ic - m_new1)
            kw_t = (k.astype(F32) * wg_col).T.astype(BF16)
            c_scr[sidx] = decay * c_prev + jnp.dot(kw_t, v_aug, preferred_element_type=F32)
            m_scr[sidx:sidx + 1, :] = m_new

    @pl.when(c == pl.num_programs(1) - 1)
    def _():
        cout_ref[0] = c_scr[...]
        mout_ref[0] = m_scr[...]


def mlstm_scan(qk, p3, gates, c_in, m_in):
    B, T, _ = qk.shape
    nc = T // M_CHUNK
    W = M_WIDTH
    fwd = lambda j: (lambda b, c: (b, c, j))
    bwd = lambda j: (lambda b, c: (b, nc - 1 - c, j))
    st4 = lambda b, c: (b, 0, 0, 0)
    st3 = lambda b, c: (b, 0, 0)
    blk = lambda w, im: pl.BlockSpec((1, M_CHUNK, w), im)
    in_specs = [blk(W, fwd(0)), blk(W, fwd(1)), blk(W, fwd(COL_MV // W)), blk(LANES, fwd(0)),
                blk(W, bwd(0)), blk(W, bwd(1)), blk(W, bwd(COL_MV // W)), blk(LANES, bwd(0)),
                pl.BlockSpec((1, 2 * M_HEADS, M_HEAD_DIM, 2 * M_HEAD_DIM), st4),
                pl.BlockSpec((1, 2 * M_HEADS, LANES), st3)]
    out_specs = [blk(W, fwd(0)), blk(W, bwd(0)),
                 pl.BlockSpec((1, 2 * M_HEADS, M_HEAD_DIM, 2 * M_HEAD_DIM), st4),
                 pl.BlockSpec((1, 2 * M_HEADS, LANES), st3)]
    out_shape = [jax.ShapeDtypeStruct((B, T, W), BF16), jax.ShapeDtypeStruct((B, T, W), BF16),
                 jax.ShapeDtypeStruct(c_in.shape, F32), jax.ShapeDtypeStruct(m_in.shape, F32)]
    return pl.pallas_call(
        _mlstm_kernel,
        grid=(B, nc),
        in_specs=in_specs, out_specs=out_specs, out_shape=out_shape,
        scratch_shapes=[pltpu.VMEM((2 * M_HEADS, M_HEAD_DIM, 2 * M_HEAD_DIM), F32),
                        pltpu.VMEM((2 * M_HEADS, LANES), F32)],
        compiler_params=_cparams(("arbitrary", "arbitrary")),
        name="mlstm_scan",
    )(qk, qk, p3, gates, qk, qk, p3, gates, c_in, m_in)


def _qkprep_kernel(q_ref, k_ref, gq_ref, gk_ref, cos_ref, sin_ref, qo_ref, ko_ref, *, do_norm, do_rope, q_scale):
    tm = q_ref.shape[1]
    lane = lax.broadcasted_iota(jnp.int32, (tm, LANES), 1)
    lo = lane < HEAD_DIM
    first_half = (lane & (HEAD_DIM // 2)) == 0

    def proc(x, g):
        if do_norm:
            ss = x * x
            s_lo = jnp.sum(jnp.where(lo, ss, 0.0), axis=1, keepdims=True)
            s_hi = jnp.sum(jnp.where(lo, 0.0, ss), axis=1, keepdims=True)
            x = (x * lax.rsqrt(jnp.where(lo, s_lo, s_hi) / HEAD_DIM + EPS)) * g
        if do_rope:
            partner = jnp.where(first_half, pltpu.roll(x, LANES - HEAD_DIM // 2, 1), pltpu.roll(x, HEAD_DIM // 2, 1))
            x = x * cos_ref[...] + partner * sin_ref[...]
        return x

    for p in range(q_ref.shape[2] // LANES):
        sl = slice(p * LANES, (p + 1) * LANES)
        qo_ref[0, :, sl] = (proc(q_ref[0, :, sl].astype(F32), gq_ref[...]) * q_scale).astype(BF16)
    ko_ref[0] = proc(k_ref[0].astype(F32), gk_ref[...]).astype(BF16)


def qk_prep(p3, col_q, col_k, gq, gk, cos, sin, do_norm, do_rope):
    B, T, _ = p3.shape
    tm = min(512, T)
    QW = N_QHEADS * HEAD_DIM
    fixed = lambda b, i: (0, 0)
    return pl.pallas_call(
        functools.partial(_qkprep_kernel, do_norm=do_norm, do_rope=do_rope, q_scale=HEAD_DIM ** -0.5),
        grid=(B, T // tm),
        in_specs=[pl.BlockSpec((1, tm, QW), lambda b, i: (b, i, col_q // QW)),
                  pl.BlockSpec((1, tm, LANES), lambda b, i: (b, i, col_k // LANES)),
                  pl.BlockSpec((1, LANES), fixed), pl.BlockSpec((1, LANES), fixed),
                  pl.BlockSpec((tm, LANES), lambda b, i: (i, 0)), pl.BlockSpec((tm, LANES), lambda b, i: (i, 0))],
        out_specs=[pl.BlockSpec((1, tm, QW), lambda b, i: (b, i, 0)),
                   pl.BlockSpec((1, tm, LANES), lambda b, i: (b, i, 0))],
        out_shape=[jax.ShapeDtypeStruct((B, T, QW), BF16), jax.ShapeDtypeStruct((B, T, LANES), BF16)],
        compiler_params=_cparams(("arbitrary", "arbitrary")),
        name="qk_prep",
    )(p3, p3, gq, gk, cos, sin)


def _stack_pair(q):
    lo = _lo_mask(q.shape)
    zero = jnp.zeros_like(q)
    return jnp.concatenate([jnp.where(lo, q, zero), jnp.where(lo, zero, q)], axis=0)


def _unstack_pair(o, tq):
    return jnp.where(_lo_mask((tq, LANES)), o[:tq], o[tq:])


def _nt_dot(a, b):
    return lax.dot_general(a, b, (((1,), (1,)), ((), ())), preferred_element_type=F32)


def _gattn_kernel(q_ref, k_ref, v_ref, kc_ref, vc_ref, o_ref, *, tk):
    tq = q_ref.shape[1]
    S = k_ref.shape[1]
    qq = _stack_pair(q_ref[0])

    def update(carry, kb, vb):
        m, l, acc = carry
        s = _nt_dot(qq, kb)
        m_new = jnp.maximum(m, jnp.max(s, axis=1, keepdims=True))
        alpha = jnp.exp(m - m_new)
        p = jnp.exp(s - m_new)
        l = alpha * l + jnp.sum(p, axis=1, keepdims=True)
        acc = alpha * acc + jnp.dot(p.astype(BF16), vb, preferred_element_type=F32)
        return m_new, l, acc

    def body(i, carry):
        start = pl.multiple_of(i * tk, tk)
        return update(carry, k_ref[0, pl.ds(start, tk), :], v_ref[0, pl.ds(start, tk), :])

    init = (jnp.full((2 * tq, 1), NEG, F32), jnp.zeros((2 * tq, 1), F32), jnp.zeros((2 * tq, LANES), F32))
    carry = lax.fori_loop(0, S // tk, body, init)
    m, l, acc = update(carry, kc_ref[0], vc_ref[0])
    o_ref[0] = _unstack_pair(acc / l, tq).astype(BF16)


def global_attention(q, k, p3, kc, p3c):
    B, S, QW = q.shape
    C = kc.shape[1]
    tq = 256
    tk = 512
    vcol = COL_GV // LANES
    return pl.pallas_call(
        functools.partial(_gattn_kernel, tk=tk),
        grid=(B, S // tq, QW // LANES),
        in_specs=[pl.BlockSpec((1, tq, LANES), lambda b, i, p: (b, i, p)),
                  pl.BlockSpec((1, S, LANES), lambda b, i, p: (b, 0, 0)),
                  pl.BlockSpec((1, S, LANES), lambda b, i, p: (b, 0, vcol)),
                  pl.BlockSpec((1, C, LANES), lambda b, i, p: (b, 0, 0)),
                  pl.BlockSpec((1, C, LANES), lambda b, i, p: (b, 0, vcol))],
        out_specs=pl.BlockSpec((1, tq, LANES), lambda b, i, p: (b, i, p)),
        out_shape=jax.ShapeDtypeStruct((B, S, QW), BF16),
        compiler_params=_cparams(("arbitrary", "arbitrary", "arbitrary")),
        name="global_attention",
    )(q, k, p3, kc, p3c)


def _wattn_kernel(sink_ref, q_ref, k_ref, v_ref, kc_ref, vc_ref, o_ref):
    tq = q_ref.shape[1]
    S = k_ref.shape[1]
    span = tq + 2 * WINDOW
    i = pl.program_id(1)
    start = pl.multiple_of(jnp.clip(i * tq - WINDOW, 0, S - span), WINDOW)
    kb = k_ref[0, pl.ds(start, span), :]
    vb = v_ref[0, pl.ds(start, span), :]
    kc = kc_ref[0]
    vc = vc_ref[0]
    qpos = i * tq + lax.broadcasted_iota(jnp.int32, (tq, span), 0)
    kpos = start + lax.broadcasted_iota(jnp.int32, (tq, span), 1)
    valid = jnp.abs(kpos - qpos) <= WINDOW
    valid2 = jnp.concatenate([valid, valid], axis=0)
    top = lax.broadcasted_iota(jnp.int32, (2 * tq, 1), 0) < tq
    for p in range(q_ref.shape[2] // LANES):
        sl = slice(p * LANES, (p + 1) * LANES)
        qq = _stack_pair(q_ref[0, :, sl])
        sink = jnp.where(top, sink_ref[PAIR_ORDER[2 * p]], sink_ref[PAIR_ORDER[2 * p + 1]])
        s_loc = jnp.where(valid2, _nt_dot(qq, kb), NEG)
        s_ctx = _nt_dot(qq, kc)
        m = jnp.maximum(jnp.maximum(jnp.max(s_loc, axis=1, keepdims=True), jnp.max(s_ctx, axis=1, keepdims=True)), sink)
        p_loc = jnp.exp(s_loc - m)
        p_ctx = jnp.exp(s_ctx - m)
        l = jnp.sum(p_loc, axis=1, keepdims=True) + jnp.sum(p_ctx, axis=1, keepdims=True) + jnp.exp(sink - m)
        o = (jnp.dot(p_loc.astype(BF16), vb, preferred_element_type=F32)
             + jnp.dot(p_ctx.astype(BF16), vc, preferred_element_type=F32)) / l
        o_ref[0, :, sl] = _unstack_pair(o, tq).astype(BF16)


def window_attention(sink, q, k, p3, p3c):
    B, S, QW = q.shape
    C = p3c.shape[1]
    tq = 256
    im = lambda j: (lambda b, i: (b, 0, j))
    return pl.pallas_call(
        _wattn_kernel,
        grid=(B, S // tq),
        in_specs=[pl.BlockSpec(memory_space=pltpu.SMEM),
                  pl.BlockSpec((1, tq, QW), lambda b, i: (b, i, 0)),
                  pl.BlockSpec((1, S, LANES), im(0)),
                  pl.BlockSpec((1, S, LANES), im(COL_WV // LANES)),
                  pl.BlockSpec((1, C, LANES), im(COL_WK // LANES)),
                  pl.BlockSpec((1, C, LANES), im(COL_WV // LANES))],
        out_specs=pl.BlockSpec((1, tq, QW), lambda b, i: (b, i, 0)),
        out_shape=jax.ShapeDtypeStruct((B, S, QW), BF16),
        compiler_params=_cparams(("arbitrary", "arbitrary")),
        name="window_attention",
    )(sink, q, k, p3, p3c, p3c)


NB_QROWS = 4
NB_KROWS = NB_QROWS + NB_ROWS


def _nattn_kernel(q_ref, k_ref, v_ref, kc_ref, vc_ref, tab_ref, o_ref):
    tq = q_ref.shape[1]
    rows = k_ref.shape[1] // GRID_W
    nk = NB_KROWS * GRID_W
    j = pl.program_id(1)
    start = pl.multiple_of(jnp.clip(NB_QROWS * j - NB_ROWS // 2, 0, rows - NB_KROWS) * GRID_W, GRID_W)
    for p in range(q_ref.shape[2] // LANES):
        sl = slice(p * LANES, (p + 1) * LANES)
        kb = k_ref[0, pl.ds(start, nk), sl]
        vb = v_ref[0, pl.ds(start, nk), sl]
        kc = kc_ref[0, :, sl]
        vc = vc_ref[0, :, sl]
        qq = _stack_pair(q_ref[0, :, sl] * (HEAD_DIM ** -0.5))
        s_loc = _nt_dot(qq, kb) + tab_ref[0, 2 * p:2 * p + 2].reshape(2 * tq, nk)
        s_ctx = _nt_dot(qq, kc)
        m = jnp.maximum(jnp.max(s_loc, axis=1, keepdims=True), jnp.max(s_ctx, axis=1, keepdims=True))
        p_loc = jnp.exp(s_loc - m)
        p_ctx = jnp.exp(s_ctx - m)
        l = jnp.sum(p_loc, axis=1, keepdims=True) + jnp.sum(p_ctx, axis=1, keepdims=True)
        o = (jnp.dot(p_loc.astype(BF16), vb, preferred_element_type=F32)
             + jnp.dot(p_ctx.astype(BF16), vc, preferred_element_type=F32)) / l
        o_ref[0, :, sl] = _unstack_pair(o, tq).astype(BF16)


def neighbourhood_bias_table(rel_bias, rows):
    nblk = rows // NB_QROWS
    tabs = []
    for jrep in (0, 1, nblk - 1):
        kstart = int(np.clip(NB_QROWS * jrep - NB_ROWS // 2, 0, rows - NB_KROWS))
        ql = np.arange(NB_QROWS * GRID_W)
        kl = np.arange(NB_KROWS * GRID_W)
        r = (NB_QROWS * jrep + ql // GRID_W)[:, None]
        c = (ql % GRID_W)[:, None]
        kr = (kstart + kl // GRID_W)[None, :]
        kc = (kl % GRID_W)[None, :]
        r0 = np.clip(r - NB_ROWS // 2, 0, rows - NB_ROWS)
        c0 = np.clip(c - NB_COLS // 2, 0, GRID_W - NB_COLS)
        valid = (kr >= r0) & (kr < r0 + NB_ROWS) & (kc >= c0) & (kc < c0 + NB_COLS)
        di = np.clip(kr - r + NB_ROWS - 1, 0, 2 * NB_ROWS - 2)
        dj = np.clip(kc - c + NB_COLS - 1, 0, 2 * NB_COLS - 2)
        tabs.append(jnp.where(valid[None], rel_bias[:, di, dj].astype(F32), NEG))
    return jnp.stack(tabs)


def neighbourhood_attention(p3, p3c, table):
    B, S, _ = p3.shape
    C = p3c.shape[1]
    QW = N_QHEADS * HEAD_DIM
    tq = NB_QROWS * GRID_W
    nblk = S // tq
    nk = NB_KROWS * GRID_W
    cls = lambda j: jnp.where(j == 0, 0, jnp.where(j == nblk - 1, 2, 1))
    im = lambda col: (lambda b, j: (b, 0, col // QW))
    return pl.pallas_call(
        _nattn_kernel,
        grid=(B, nblk),
        in_specs=[pl.BlockSpec((1, tq, QW), lambda b, j: (b, j, COL_NQ // QW)),
                  pl.BlockSpec((1, S, QW), im(COL_NK)), pl.BlockSpec((1, S, QW), im(COL_NV)),
                  pl.BlockSpec((1, C, QW), im(COL_NK)), pl.BlockSpec((1, C, QW), im(COL_NV)),
                  pl.BlockSpec((1, N_QHEADS, tq, nk), lambda b, j: (cls(j), 0, 0, 0))],
        out_specs=pl.BlockSpec((1, tq, QW), lambda b, j: (b, j, 0)),
        out_shape=jax.ShapeDtypeStruct((B, S, QW), BF16),
        compiler_params=_cparams(("arbitrary", "arbitrary")),
        name="neighbourhood_attention",
    )(p3, p3, p3, p3c, p3c, table)


def _cattn_kernel(sink_ref, q_ref, k_ref, v_ref, o_ref, *, kv_tiles, use_sink, q_scale):
    C = q_ref.shape[1]
    top = lax.broadcasted_iota(jnp.int32, (2 * C, 1), 0) < C
    for p in range(q_ref.shape[2] // LANES):
        sl = slice(p * LANES, (p + 1) * LANES)
        ksl = sl if kv_tiles > 1 else slice(0, LANES)
        q = q_ref[0, :, sl]
        if q_scale != 1.0:
            q = q * q_scale
        s = _nt_dot(_stack_pair(q), k_ref[0, :, ksl])
        m = jnp.max(s, axis=1, keepdims=True)
        if use_sink:
            sink = jnp.where(top, sink_ref[PAIR_ORDER[2 * p]], sink_ref[PAIR_ORDER[2 * p + 1]])
            m = jnp.maximum(m, sink)
        e = jnp.exp(s - m)
        l = jnp.sum(e, axis=1, keepdims=True)
        if use_sink:
            l = l + jnp.exp(sink - m)
        o = jnp.dot(e.astype(BF16), v_ref[0, :, ksl], preferred_element_type=F32) / l
        o_ref[0, :, sl] = _unstack_pair(o, C).astype(BF16)


def context_attention(sink, q_arr, q_col, k_arr, k_col, v_arr, v_col, kv_tiles, use_sink, q_scale):
    B, C, _ = q_arr.shape
    QW = N_QHEADS * HEAD_DIM
    KW = kv_tiles * LANES
    return pl.pallas_call(
        functools.partial(_cattn_kernel, kv_tiles=kv_tiles, use_sink=use_sink, q_scale=q_scale),
        grid=(B,),
        in_specs=[pl.BlockSpec(memory_space=pltpu.SMEM),
                  pl.BlockSpec((1, C, QW), lambda b: (b, 0, q_col // QW)),
                  pl.BlockSpec((1, C, KW), lambda b: (b, 0, k_col // KW)),
                  pl.BlockSpec((1, C, KW), lambda b: (b, 0, v_col // KW))],
        out_specs=pl.BlockSpec((1, C, QW), lambda b: (b, 0, 0)),
        out_shape=jax.ShapeDtypeStruct((B, C, QW), BF16),
        compiler_params=_cparams(("arbitrary",)),
        name="context_attention",
    )(sink, q_arr, k_arr, v_arr)


def _merge_kernel(h0_ref, h1_ref, mo_ref, ng_ref, ab_ref, ac_ref, ad_ref, g0_ref, g1_ref, g2_ref, g3_ref,
                  wb_ref, wo_ref, x_ref, ga_ref, xo_ref):
    hs = h0_ref[...].astype(F32) + h1_ref[...].astype(F32)
    parts = []
    for h in range(M_HEADS):
        hh = hs[:, h * M_HEAD_DIM:(h + 1) * M_HEAD_DIM]
        parts.append(hh * lax.rsqrt(jnp.mean(hh * hh, axis=1, keepdims=True) + EPS))
    a0 = (jnp.concatenate(parts, axis=1) * ng_ref[...]) * _sigmoid(mo_ref[...].astype(F32))
    branches = (a0.astype(BF16), ab_ref[...], ac_ref[...], ad_ref[...])
    gates = (g0_ref, g1_ref, g2_ref, g3_ref)
    y = None
    for i in range(N_BRANCH):
        t = _sigmoid(gates[i][...].astype(F32)) * jnp.dot(branches[i], wb_ref[i], preferred_element_type=F32)
        y = t if y is None else y + t
    out = jnp.dot(y.astype(BF16), wo_ref[...], preferred_element_type=F32)
    xo_ref[...] = x_ref[...] + ga_ref[0] * out


def merge_branches(h0, h1, p2, norm_g, att_b, att_c, att_d, w_branch, w_out, x, ga, rows_per_mod):
    R, D = x.shape
    tm = min(512, R)
    tpm = rows_per_mod // tm
    row = lambda i: (i, 0)
    fixed = lambda i: (0, 0)
    gate_spec = lambda k: pl.BlockSpec((tm, D), lambda i: (i, COL_GATE // D + k))
    return pl.pallas_call(
        _merge_kernel,
        grid=(R // tm,),
        in_specs=[pl.BlockSpec((tm, M_WIDTH), row), pl.BlockSpec((tm, M_WIDTH), row),
                  pl.BlockSpec((tm, M_WIDTH), lambda i: (i, COL_MO // M_WIDTH)),
                  pl.BlockSpec((1, M_WIDTH), fixed),
                  pl.BlockSpec((tm, BRANCH_WIDTH), row), pl.BlockSpec((tm, BRANCH_WIDTH), row),
                  pl.BlockSpec((tm, BRANCH_WIDTH), row),
                  gate_spec(0), gate_spec(1), gate_spec(2), gate_spec(3),
                  pl.BlockSpec((N_BRANCH, BRANCH_WIDTH, D), lambda i: (0, 0, 0)),
                  pl.BlockSpec((D, D), fixed),
                  pl.BlockSpec((tm, D), row),
                  pl.BlockSpec((1, 1, D), lambda i: (i // tpm, 0, 0))],
        out_specs=pl.BlockSpec((tm, D), row),
        out_shape=jax.ShapeDtypeStruct((R, D), F32),
        compiler_params=_cparams(("arbitrary",)),
        name="merge_branches",
    )(h0, h1, p2, norm_g.reshape(1, -1), att_b, att_c, att_d, p2, p2, p2, p2, w_branch, w_out, x, ga)


def _router_kernel(x_ref, g_ref, sc_ref, sh_ref, wr_ref, rb_ref, h_ref, e_ref, w_ref):
    x = x_ref[...]
    y = x * lax.rsqrt(jnp.mean(x * x, axis=-1, keepdims=True) + EPS)
    h = (y * g_ref[...]) * (1.0 + sc_ref[0]) + sh_ref[0]
    h_ref[...] = h.astype(BF16)
    logits = lax.dot_general(wr_ref[...], h, (((1,), (1,)), ((), ())), precision=HIGHEST,
                             preferred_element_type=F32)
    scores = _sigmoid(logits)
    sel = scores + rb_ref[...]
    E = EXPERTS_PER_GROUP
    tm = x.shape[0]
    sub = lax.broadcasted_iota(jnp.int32, (E, tm), 0)
    best = None
    for g in range(N_GROUPS):
        v = sel[g * E:(g + 1) * E]
        sc = scores[g * E:(g + 1) * E]
        m1 = jnp.max(v, axis=0, keepdims=True)
        i1 = jnp.min(jnp.where(v == m1, sub, E), axis=0, keepdims=True)
        rest = jnp.where(sub == i1, -jnp.inf, v)
        m2 = jnp.max(rest, axis=0, keepdims=True)
        i2 = jnp.min(jnp.where(rest == m2, sub, E), axis=0, keepdims=True)
        s1 = jnp.sum(jnp.where(sub == i1, sc, 0.0), axis=0, keepdims=True)
        s2 = jnp.sum(jnp.where(sub == i2, sc, 0.0), axis=0, keepdims=True)
        cand = (m1 + m2, g * E + i1, g * E + i2, s1, s2)
        if best is None:
            best = cand
        else:
            take = cand[0] > best[0]
            best = tuple(jnp.where(take, cn, bs) for cn, bs in zip(cand, best))
    _, e1, e2, s1, s2 = best
    tot = s1 + s2
    e_ref[0:1, :] = e1
    e_ref[1:2, :] = e2
    w_ref[0:1, :] = s1 / tot
    w_ref[1:2, :] = s2 / tot


def router(x, g, sc, sh, w_router_t, router_bias, rows_per_mod):
    R, D = x.shape
    tm = min(512, R)
    tpm = rows_per_mod // tm
    mod = lambda i: (i // tpm, 0, 0)
    fixed = lambda i: (0, 0)
    return pl.pallas_call(
        _router_kernel,
        grid=(R // tm,),
        in_specs=[pl.BlockSpec((tm, D), lambda i: (i, 0)), pl.BlockSpec((1, D), fixed),
                  pl.BlockSpec((1, 1, D), mod), pl.BlockSpec((1, 1, D), mod),
                  pl.BlockSpec((N_EXPERTS, D), fixed), pl.BlockSpec((N_EXPERTS, 1), fixed)],
        out_specs=[pl.BlockSpec((tm, D), lambda i: (i, 0)), pl.BlockSpec((2, tm), lambda i: (0, i)),
                   pl.BlockSpec((2, tm), lambda i: (0, i))],
        out_shape=[jax.ShapeDtypeStruct((R, D), BF16), jax.ShapeDtypeStruct((2, R), jnp.int32),
                   jax.ShapeDtypeStruct((2, R), F32)],
        compiler_params=_cparams(("arbitrary",)),
        name="router",
    )(x, g.reshape(1, D), sc, sh, w_router_t, router_bias.reshape(-1, 1))


def _expert_kernel(be_ref, nu_ref, xs_ref, wg_ref, wu_ref, wd_ref, ys_ref):
    del be_ref
    used = pl.program_id(0) < nu_ref[0]

    @pl.when(used)
    def _():
        x = xs_ref[...]
        g = jnp.dot(x, wg_ref[0], preferred_element_type=F32)
        u = jnp.dot(x, wu_ref[0], preferred_element_type=F32)
        hmid = (g * _sigmoid(g)) * u
        ys_ref[...] = jnp.dot(hmid.astype(BF16), wd_ref[0], preferred_element_type=F32)

    @pl.when(jnp.logical_not(used))
    def _():
        ys_ref[...] = jnp.zeros_like(ys_ref)


def expert_ffn(blk_e, n_used, xs, e_gate, e_up, e_down):
    cap, D = xs.shape
    nblk = cap // MOE_BLOCK
    wmap = lambda i, be, nu: (be[i], 0, 0)
    grid_spec = pltpu.PrefetchScalarGridSpec(
        num_scalar_prefetch=2,
        grid=(nblk,),
        in_specs=[pl.BlockSpec((MOE_BLOCK, D), lambda i, be, nu: (i, 0)),
                  pl.BlockSpec((1, D, D_EXPERT), wmap), pl.BlockSpec((1, D, D_EXPERT), wmap),
                  pl.BlockSpec((1, D_EXPERT, D), wmap)],
        out_specs=pl.BlockSpec((MOE_BLOCK, D), lambda i, be, nu: (i, 0)),
    )
    return pl.pallas_call(
        _expert_kernel,
        grid_spec=grid_spec,
        out_shape=jax.ShapeDtypeStruct((cap, D), F32),
        compiler_params=_cparams(("arbitrary",)),
        name="expert_ffn",
    )(blk_e, n_used, xs, e_gate, e_up, e_down)


def moe_ffn(h2, expert, weight, e_gate, e_up, e_down):
    T, D = h2.shape
    n_assign = 2 * T
    e_flat = expert.reshape(-1)
    tok = jnp.tile(jnp.arange(T, dtype=jnp.int32), 2)
    order = jnp.argsort(e_flat, stable=True)
    e_s = e_flat[order]
    counts = jnp.bincount(e_flat, length=N_EXPERTS)
    starts = jnp.cumsum(counts) - counts
    padded = ((counts + MOE_BLOCK - 1) // MOE_BLOCK) * MOE_BLOCK
    pends = jnp.cumsum(padded)
    pstarts = pends - padded
    dest = (pstarts[e_s] + (jnp.arange(n_assign) - starts[e_s])).astype(jnp.int32)
    nblk = -(-(n_assign + N_EXPERTS * (MOE_BLOCK - 1)) // MOE_BLOCK)
    cap = nblk * MOE_BLOCK
    slot_tok = jnp.zeros((cap,), jnp.int32).at[dest].set(tok[order])
    slot_of = jnp.zeros((n_assign,), jnp.int32).at[order].set(dest).reshape(2, T)
    blk_e = jnp.minimum(jnp.searchsorted(pends, jnp.arange(nblk) * MOE_BLOCK, side='right'),
                        N_EXPERTS - 1).astype(jnp.int32)
    n_used = (pends[-1] // MOE_BLOCK).astype(jnp.int32).reshape(1)
    xs = h2[slot_tok]
    ys = expert_ffn(blk_e, n_used, xs, e_gate, e_up, e_down)
    return ys[slot_of[0]] * weight[0][:, None] + ys[slot_of[1]] * weight[1][:, None]


def _final_kernel(x_ref, f_ref, ga_ref, g_ref, o_ref):
    x = x_ref[...] + ga_ref[0] * f_ref[...]
    o_ref[...] = (x * lax.rsqrt(jnp.mean(x * x, axis=-1, keepdims=True) + EPS)) * g_ref[...]


def final_norm(x, f, ga, g, rows_per_mod):
    R, D = x.shape
    tm = min(1024, R)
    tpm = rows_per_mod // tm
    row = lambda i: (i, 0)
    return pl.pallas_call(
        _final_kernel,
        grid=(R // tm,),
        in_specs=[pl.BlockSpec((tm, D), row), pl.BlockSpec((tm, D), row),
                  pl.BlockSpec((1, 1, D), lambda i: (i // tpm, 0, 0)), pl.BlockSpec((1, D), lambda i: (0, 0))],
        out_specs=pl.BlockSpec((tm, D), row),
        out_shape=jax.ShapeDtypeStruct((R, D), F32),
        compiler_params=_cparams(("arbitrary",)),
        name="final_norm",
    )(x, f, ga, g.reshape(1, D))


_DEINTERLEAVE = np.concatenate([np.arange(0, HEAD_DIM, 2), np.arange(1, HEAD_DIM, 2)])


def _head_cols(n_heads, order, perm):
    return np.concatenate([h * HEAD_DIM + perm for h in order])


def _reorder_w_in(w_in):
    widths = (512, 512, 512, 512, 16, 512, 128, 128, 512, 128, 128, 512, 512, 512, 4096)
    offs = np.concatenate([[0], np.cumsum(widths)])
    (mq, mk, mv, mo, mif, gq, gk, gv, wq, wk, wv, nq, nk, nv, gate) = [
        w_in[:, offs[i]:offs[i + 1]] for i in range(len(widths))]
    qperm = _head_cols(N_QHEADS, PAIR_ORDER, _DEINTERLEAVE)
    kperm = _head_cols(2, (0, 1), _DEINTERLEAVE)
    w = jnp.concatenate([mq, mk, mv, mo, gq[:, qperm], wq[:, qperm], nq, nk, nv,
                         gk[:, kperm], gv, wk[:, kperm], wv, gate], axis=1).astype(BF16)
    wif = jnp.pad(mif, ((0, 0), (0, LANES - mif.shape[1]))).astype(BF16)
    return w, wif


def _rope_tables(S):
    t = np.arange(S)
    n_freq = HEAD_DIM // 4
    inv_freq = jnp.asarray(ROPE_THETA, F32) ** (-jnp.arange(n_freq, dtype=F32) / n_freq)
    row = jnp.asarray(t // GRID_W, F32)
    col = jnp.asarray(t % GRID_W, F32)
    ang = jnp.concatenate([row[:, None] * inv_freq, col[:, None] * inv_freq], axis=-1)
    cos, sin = jnp.cos(ang), jnp.sin(ang)
    cos_t = jnp.tile(jnp.concatenate([cos, cos], axis=1), (1, 2))
    sin_t = jnp.tile(jnp.concatenate([-sin, sin], axis=1), (1, 2))
    return cos_t, sin_t


def kernel(x, c, ctx, c_ctx, w_mod, b_mod, norm1_g, norm2_g, w_in, m_conv_w, m_conv_b, m_gate_b, m_norm_g,
           g_qnorm, g_knorm, w_sink, n_rel_bias, w_branch, w_out, w_router, router_bias, e_gate, e_up, e_down,
           final_g):
    B, S, D = x.shape
    C = ctx.shape[1]
    L = w_mod.shape[0]
    R, RC = B * S, B * C
    rows = S // GRID_W

    c_all = jnp.zeros((16, D), F32).at[:B].set(c).at[B].set(c_ctx)
    mod = modulation(c_all, w_mod, b_mod).reshape(L, 16, 6, D)
    cos_t, sin_t = _rope_tables(S)
    order_rows = _head_cols(N_QHEADS, PAIR_ORDER, np.arange(HEAD_DIM))
    w_router_t = w_router.T

    xl = x.reshape(R, D)
    xc = ctx.reshape(RC, D)
    f_lat = f_ctx = ga2_lat = ga2_ctx = None
    for l in range(L):
        need_ctx = l < L - 1
        lat = [mod[l, :B, i].reshape(B, 1, D) for i in range(6)]
        cx = [mod[l, B:B + 1, i].reshape(1, 1, D) for i in range(6)]
        w, wif = _reorder_w_in(w_in[l])
        gb = jnp.pad(m_gate_b[l].reshape(1, -1), ((0, 0), (0, LANES - 16)))
        gq = jnp.tile(g_qnorm[l][_DEINTERLEAVE], 2).reshape(1, LANES)
        gk = jnp.tile(g_knorm[l][_DEINTERLEAVE], 2).reshape(1, LANES)
        wb = w_branch[l].at[1].set(w_branch[l][1][order_rows]).at[2].set(w_branch[l][2][order_rows]).astype(BF16)
        wo = w_out[l].astype(BF16)

        if l == 0:
            p_lat, g_lat = in_projection(xl, norm1_g[l], lat[1], lat[0], w, wif, gb, S)
            p_ctx, g_ctx = in_projection(xc, norm1_g[l], cx[1], cx[0], w, wif, gb, RC)
        else:
            xl, p_lat, g_lat = in_projection(xl, norm1_g[l], lat[1], lat[0], w, wif, gb, S, f=f_lat, ga=ga2_lat)
            xc, p_ctx, g_ctx = in_projection(xc, norm1_g[l], cx[1], cx[0], w, wif, gb, RC, f=f_ctx, ga=ga2_ctx)
        p3 = p_lat.reshape(B, S, -1)
        p3c = p_ctx.reshape(B, C, -1)

        qk_c = mlstm_conv(p3c, m_conv_w[l], m_conv_b[l])
        qk_l = mlstm_conv(p3, m_conv_w[l], m_conv_b[l])
        c0 = jnp.zeros((B, 2 * M_HEADS, M_HEAD_DIM, 2 * M_HEAD_DIM), F32)
        m0 = jnp.full((B, 2 * M_HEADS, LANES), M_INIT, F32)
        hc0, hc1, c1, m1 = mlstm_scan(qk_c, p3c, g_ctx.reshape(B, C, LANES), c0, m0)
        hl0, hl1, _, _ = mlstm_scan(qk_l, p3, g_lat.reshape(B, S, LANES), c1, m1)

        gq_l, gk_l = qk_prep(p3, COL_GQ, COL_GK, gq, gk, cos_t, sin_t, True, True)
        gq_c, gk_c = qk_prep(p3c, COL_GQ, COL_GK, gq, gk, cos_t[:C], sin_t[:C], True, False)
        att_b = global_attention(gq_l, gk_l, p3, gk_c, p3c)

        wq_l, wk_l = qk_prep(p3, COL_WQ, COL_WK, gq, gk, cos_t, sin_t, False, True)
        att_c = window_attention(w_sink[l], wq_l, wk_l, p3, p3c)

        table = neighbourhood_bias_table(n_rel_bias[l], rows)
        att_d = neighbourhood_attention(p3, p3c, table)

        xl = merge_branches(hl0.reshape(R, -1), hl1.reshape(R, -1), p_lat, m_norm_g[l], att_b.reshape(R, -1),
                            att_c.reshape(R, -1), att_d.reshape(R, -1), wb, wo, xl, lat[2], S)
        h2, ex, wt = router(xl, norm2_g[l], lat[4], lat[3], w_router_t, router_bias, S)
        if need_ctx:
            scale = HEAD_DIM ** -0.5
            cb = context_attention(w_sink[l], gq_c, 0, gk_c, 0, p3c, COL_GV, 1, False, 1.0)
            cc = context_attention(w_sink[l], p3c, COL_WQ, p3c, COL_WK, p3c, COL_WV, 1, True, scale)
            cd = context_attention(w_sink[l], p3c, COL_NQ, p3c, COL_NK, p3c, COL_NV, 4, False, scale)
            xc = merge_branches(hc0.reshape(RC, -1), hc1.reshape(RC, -1), p_ctx, m_norm_g[l], cb.reshape(RC, -1),
                                cc.reshape(RC, -1), cd.reshape(RC, -1), wb, wo, xc, cx[2], RC)
            h2c, exc, wtc = router(xc, norm2_g[l], cx[4], cx[3], w_router_t, router_bias, RC)
            h2 = jnp.concatenate([h2, h2c], axis=0)
            ex = jnp.concatenate([ex, exc], axis=1)
            wt = jnp.concatenate([wt, wtc], axis=1)
        f = moe_ffn(h2, ex, wt, e_gate[l].astype(BF16), e_up[l].astype(BF16), e_down[l].astype(BF16))
        f_lat, ga2_lat = f[:R], lat[5]
        if need_ctx:
            f_ctx, ga2_ctx = f[R:], cx[5]
    out = final_norm(xl, f_lat, ga2_lat, final_g, S)
    return out.reshape(B, S, D)
```

```python
import functools

import numpy as np
import jax
import jax.numpy as jnp
from jax import lax
from jax.experimental import pallas as pl
from jax.experimental.pallas import tpu as pltpu

F32 = jnp.float32
BF16 = jnp.bfloat16
HIGHEST = lax.Precision.HIGHEST

D_MODEL = 1024
DEPTH = 2
GRID_W = 64
HEAD_DIM = 64
ROPE_THETA = 10000.0
EPS = 1e-6
M_INIT = -1e30
NEG = -1e30
M_HEADS = 4
M_HEAD_DIM = 128
M_WIDTH = M_HEADS * M_HEAD_DIM
M_CHUNK = 128
N_QHEADS = 8
WINDOW = 128
NB_ROWS = 8
NB_COLS = 16
N_BRANCH = 4
BRANCH_WIDTH = 512
N_EXPERTS = 32
N_GROUPS = 4
EXPERTS_PER_GROUP = N_EXPERTS // N_GROUPS
D_EXPERT = 512
MOE_BLOCK = 256
LANES = 128
VMEM_LIMIT = 56 * 1024 * 1024

COL_MQ, COL_MK, COL_MV, COL_MO = 0, 512, 1024, 1536
COL_GQ, COL_WQ, COL_NQ, COL_NK, COL_NV = 2048, 2560, 3072, 3584, 4096
COL_GK, COL_GV, COL_WK, COL_WV = 4608, 4736, 4864, 4992
COL_GATE = 5120
N_PROJ_OUT = COL_GATE + N_BRANCH * D_MODEL
PAIR_ORDER = (0, 4, 1, 5, 2, 6, 3, 7)


def _cparams(sem):
    return pltpu.CompilerParams(dimension_semantics=sem, vmem_limit_bytes=VMEM_LIMIT)


def _sigmoid(x):
    return 1.0 / (1.0 + jnp.exp(-x))


def _lo_mask(shape):
    return lax.broadcasted_iota(jnp.int32, shape, len(shape) - 1) < HEAD_DIM


def _mod_kernel(c_ref, w_ref, b_ref, o_ref):
    c = c_ref[...]
    a = c * _sigmoid(c)
    o_ref[0] = jnp.dot(a, w_ref[0], precision=HIGHEST, preferred_element_type=F32) + b_ref[0]


def modulation(c_all, w_mod, b_mod):
    L, D, N = w_mod.shape
    R = c_all.shape[0]
    tn = 1536
    return pl.pallas_call(
        _mod_kernel,
        grid=(L, N // tn),
        in_specs=[pl.BlockSpec((R, D), lambda l, j: (0, 0)),
                  pl.BlockSpec((1, D, tn), lambda l, j: (l, 0, j)),
                  pl.BlockSpec((1, 1, tn), lambda l, j: (l, 0, j))],
        out_specs=pl.BlockSpec((1, R, tn), lambda l, j: (l, 0, j)),
        out_shape=jax.ShapeDtypeStruct((L, R, N), F32),
        compiler_params=_cparams(("arbitrary", "arbitrary")),
        name="modulation",
    )(c_all, w_mod, b_mod.reshape(L, 1, N))


def _log_sigmoid(z):
    return jnp.minimum(z, 0.0) - jnp.log(1.0 + jnp.exp(-jnp.abs(z)))


def _moe_combine(y0_ref, y1_ref, wt_ref):
    wt = wt_ref[...]
    return wt[:, 0:1] * y0_ref[...] + wt[:, 1:2] * y1_ref[...]


def _inproj_kernel(*refs, has_f):
    if has_f:
        (x_ref, y0_ref, y1_ref, wt_ref, ga_ref, g_ref, sc_ref, sh_ref, w_ref, wif_ref, gb_ref,
         xo_ref, p_ref, gate_ref, h_scr) = refs
    else:
        x_ref, g_ref, sc_ref, sh_ref, w_ref, wif_ref, gb_ref, p_ref, gate_ref, h_scr = refs

    @pl.when(pl.program_id(1) == 0)
    def _():
        x = x_ref[...]
        if has_f:
            x = x + ga_ref[0] * _moe_combine(y0_ref, y1_ref, wt_ref)
            xo_ref[...] = x
        y = x * lax.rsqrt(jnp.mean(x * x, axis=-1, keepdims=True) + EPS)
        h = (y * g_ref[...]) * (1.0 + sc_ref[0]) + sh_ref[0]
        hb = h.astype(BF16)
        h_scr[...] = hb
        z = jnp.dot(hb, wif_ref[...], preferred_element_type=F32) + gb_ref[...]
        lane = lax.broadcasted_iota(jnp.int32, z.shape, 1)
        is_forget = (lane & 0xF4) == 4
        gate_ref[...] = jnp.where(is_forget, _log_sigmoid(z), z)

    p_ref[...] = jnp.dot(h_scr[...], w_ref[...], preferred_element_type=F32).astype(BF16)


def _moe_specs(moe, tm, nidx):
    y, wt, row0 = moe
    T, D = wt.shape[0], y.shape[1]
    assert row0 % tm == 0 and T % tm == 0, (row0, T, tm)
    b0, b1 = row0 // tm, (T + row0) // tm
    if nidx == 2:
        maps = (lambda i, j: (b0 + i, 0), lambda i, j: (b1 + i, 0))
    else:
        maps = (lambda i: (b0 + i, 0), lambda i: (b1 + i, 0))
    return ([pl.BlockSpec((tm, D), maps[0]), pl.BlockSpec((tm, D), maps[1]), pl.BlockSpec((tm, 2), maps[0])],
            [y, y, wt])


def in_projection(x, g, sc, sh, w, wif, gb, rows_per_mod, moe=None, ga=None):
    R, D = x.shape
    N = w.shape[1]
    tm = min(1024, R)
    tn = 1024
    tpm = rows_per_mod // tm
    has_f = moe is not None
    row = lambda i, j: (i, 0)
    mod = lambda i, j: (i // tpm, 0, 0)
    fixed = lambda i, j: (0, 0)
    in_specs = [pl.BlockSpec((tm, D), row)]
    args = [x]
    if has_f:
        moe_specs, moe_args = _moe_specs(moe, tm, 2)
        in_specs += moe_specs + [pl.BlockSpec((1, 1, D), mod)]
        args += moe_args + [ga]
    in_specs += [pl.BlockSpec((1, D), fixed), pl.BlockSpec((1, 1, D), mod), pl.BlockSpec((1, 1, D), mod),
                 pl.BlockSpec((D, tn), lambda i, j: (0, j)), pl.BlockSpec((D, LANES), fixed),
                 pl.BlockSpec((1, LANES), fixed)]
    args += [g.reshape(1, D), sc, sh, w, wif, gb]
    out_specs = [pl.BlockSpec((tm, tn), lambda i, j: (i, j)), pl.BlockSpec((tm, LANES), row)]
    out_shape = [jax.ShapeDtypeStruct((R, N), BF16), jax.ShapeDtypeStruct((R, LANES), F32)]
    if has_f:
        out_specs = [pl.BlockSpec((tm, D), row)] + out_specs
        out_shape = [jax.ShapeDtypeStruct((R, D), F32)] + out_shape
    return pl.pallas_call(
        functools.partial(_inproj_kernel, has_f=has_f),
        grid=(R // tm, N // tn),
        in_specs=in_specs, out_specs=out_specs, out_shape=out_shape,
        scratch_shapes=[pltpu.VMEM((tm, D), BF16)],
        compiler_params=_cparams(("arbitrary", "arbitrary")),
        name="in_projection",
    )(*args)


def _mconv_kernel(x_ref, w_ref, b_ref, o_ref, *, k_scale):
    x = x_ref[0].astype(F32)
    T = x.shape[0]
    row = lax.broadcasted_iota(jnp.int32, x.shape, 0)
    x_prev = jnp.where(row == 0, 0.0, pltpu.roll(x, 1, 0))
    x_next = jnp.where(row == T - 1, 0.0, pltpu.roll(x, T - 1, 0))
    y = x_prev * w_ref[0:1, :] + x * w_ref[1:2, :] + x_next * w_ref[2:3, :] + b_ref[...]
    y = y * _sigmoid(y)
    scale = jnp.where(pl.program_id(1) >= pl.num_programs(1) // 2, k_scale, 1.0)
    o_ref[0] = (y * scale).astype(BF16)


def mlstm_conv(p3, conv_w, conv_b):
    B, T, _ = p3.shape
    tc = 256
    nct = 2 * M_WIDTH // tc
    return pl.pallas_call(
        functools.partial(_mconv_kernel, k_scale=M_HEAD_DIM ** -0.5),
        grid=(B, nct),
        in_specs=[pl.BlockSpec((1, T, tc), lambda b, j: (b, 0, j)),
                  pl.BlockSpec((3, tc), lambda b, j: (0, j)),
                  pl.BlockSpec((1, tc), lambda b, j: (0, j))],
        out_specs=pl.BlockSpec((1, T, tc), lambda b, j: (b, 0, j)),
        out_shape=jax.ShapeDtypeStruct((B, T, 2 * M_WIDTH), BF16),
        compiler_params=_cparams(("arbitrary", "arbitrary")),
        name="mlstm_conv",
    )(p3, conv_w, conv_b.reshape(1, -1))


def _mlstm_kernel(q0_ref, k0_ref, v0_ref, g0_ref, q1_ref, k1_ref, v1_ref, g1_ref, cin_ref, min_ref,
                  h0_ref, h1_ref, cout_ref, mout_ref, c_scr, m_scr):
    c = pl.program_id(1)
    L = M_CHUNK

    @pl.when(c == 0)
    def _():
        c_scr[...] = cin_ref[0]
        m_scr[...] = min_ref[0]

    row = lax.broadcasted_iota(jnp.int32, (L, L), 0)
    col = lax.broadcasted_iota(jnp.int32, (L, L), 1)
    ones_col = jnp.where(col == 0, 1.0, 0.0).astype(BF16)
    all_ones = jnp.ones((L, L), F32)
    dirs = ((q0_ref, k0_ref, v0_ref, g0_ref, h0_ref), (q1_ref, k1_ref, v1_ref, g1_ref, h1_ref))
    for d, (q_ref, k_ref, v_ref, g_ref, h_ref) in enumerate(dirs):
        G = g_ref[0]
        tri = jnp.where(col <= row, 1.0, 0.0) if d == 0 else jnp.where(col >= row, 1.0, 0.0)
        CS = jnp.dot(tri, G, precision=HIGHEST, preferred_element_type=F32)
        TOT = jnp.dot(all_ones, G, precision=HIGHEST, preferred_element_type=F32)
        GT, CST, TOTT = G.T, CS.T, TOT.T
        keep = (col <= row) if d == 0 else (col >= row)
        for h in range(M_HEADS):
            ci, cf, sidx = d * 8 + h, d * 8 + 4 + h, d * M_HEADS + h
            bc, br = CS[:, cf:cf + 1], CST[cf:cf + 1, :]
            ic, ir = G[:, ci:ci + 1], GT[ci:ci + 1, :]
            bend_c, bend_r = TOT[:, cf:cf + 1], TOTT[cf:cf + 1, :]
            m_prev = m_scr[sidx:sidx + 1, :]
            m_prev1 = m_prev[:, 0:1]
            hs = slice(h * M_HEAD_DIM, (h + 1) * M_HEAD_DIM)
            q = q_ref[0, :, hs]
            k = k_ref[0, :, hs]
            v_aug = jnp.concatenate([v_ref[0, :, hs], ones_col], axis=1)
            c_prev = c_scr[sidx]

            dmat = jnp.where(keep, bc - br + ir, NEG)
            m_inter = bc + m_prev1
            m_t = jnp.maximum(jnp.max(dmat, axis=1, keepdims=True), m_inter)
            s_qk = lax.dot_general(q, k, (((1,), (1,)), ((), ())), preferred_element_type=F32)
            a = s_qk * jnp.exp(dmat - m_t)
            inter = jnp.exp(m_inter - m_t)
            nd = (jnp.dot(a.astype(BF16), v_aug, preferred_element_type=F32)
                  + inter * jnp.dot(q, c_prev.astype(BF16), preferred_element_type=F32))
            den = nd[:, M_HEAD_DIM:M_HEAD_DIM + 1]
            hval = nd[:, :M_HEAD_DIM] / jnp.maximum(jnp.abs(den), jnp.exp(-m_t))
            h_ref[0, :, hs] = hval.astype(BF16)

            g_row = bend_r - br + ir
            m_new = jnp.maximum(bend_r + m_prev, jnp.max(g_row, axis=1, keepdims=True))
            m_new1 = m_new[:, 0:1]
            decay = jnp.exp(bend_r[:, 0:1] + m_prev1 - m_new1)
            wg_col = jnp.exp(bend_c - bc + ic - m_new1)
            kw_t = (k.astype(F32) * wg_col).T.astype(BF16)
            c_scr[sidx] = decay * c_prev + jnp.dot(kw_t, v_aug, preferred_element_type=F32)
            m_scr[sidx:sidx + 1, :] = m_new

    @pl.when(c == pl.num_programs(1) - 1)
    def _():
        cout_ref[0] = c_scr[...]
        mout_ref[0] = m_scr[...]


def mlstm_scan(qk, p3, gates, c_in, m_in):
    B, T, _ = qk.shape
    nc = T // M_CHUNK
    W = M_WIDTH
    fwd = lambda j: (lambda b, c: (b, c, j))
    bwd = lambda j: (lambda b, c: (b, nc - 1 - c, j))
    st4 = lambda b, c: (b, 0, 0, 0)
    st3 = lambda b, c: (b, 0, 0)
    blk = lambda w, im: pl.BlockSpec((1, M_CHUNK, w), im)
    in_specs = [blk(W, fwd(0)), blk(W, fwd(1)), blk(W, fwd(COL_MV // W)), blk(LANES, fwd(0)),
                blk(W, bwd(0)), blk(W, bwd(1)), blk(W, bwd(COL_MV // W)), blk(LANES, bwd(0)),
                pl.BlockSpec((1, 2 * M_HEADS, M_HEAD_DIM, 2 * M_HEAD_DIM), st4),
                pl.BlockSpec((1, 2 * M_HEADS, LANES), st3)]
    out_specs = [blk(W, fwd(0)), blk(W, bwd(0)),
                 pl.BlockSpec((1, 2 * M_HEADS, M_HEAD_DIM, 2 * M_HEAD_DIM), st4),
                 pl.BlockSpec((1, 2 * M_HEADS, LANES), st3)]
    out_shape = [jax.ShapeDtypeStruct((B, T, W), BF16), jax.ShapeDtypeStruct((B, T, W), BF16),
                 jax.ShapeDtypeStruct(c_in.shape, F32), jax.ShapeDtypeStruct(m_in.shape, F32)]
    return pl.pallas_call(
        _mlstm_kernel,
        grid=(B, nc),
        in_specs=in_specs, out_specs=out_specs, out_shape=out_shape,
        scratch_shapes=[pltpu.VMEM((2 * M_HEADS, M_HEAD_DIM, 2 * M_HEAD_DIM), F32),
                        pltpu.VMEM((2 * M_HEADS, LANES), F32)],
        compiler_params=_cparams(("arbitrary", "arbitrary")),
        name="mlstm_scan",
    )(qk, qk, p3, gates, qk, qk, p3, gates, c_in, m_in)


VT_ROWS = LANES + 16


def _qkprep_kernel(*refs, do_norm, do_rope, q_scale, with_vt):
    if with_vt:
        q_ref, k_ref, v_ref, gq_ref, gk_ref, cos_ref, sin_ref, qo_ref, ko_ref, vt_ref = refs
        vt_ref[0, 0, 0:LANES, :] = v_ref[0].astype(F32).T.astype(BF16)
        vt_ref[0, 0, LANES:VT_ROWS, :] = jnp.ones((VT_ROWS - LANES, v_ref.shape[1]), BF16)
    else:
        q_ref, k_ref, gq_ref, gk_ref, cos_ref, sin_ref, qo_ref, ko_ref = refs
    tm = q_ref.shape[1]
    lane = lax.broadcasted_iota(jnp.int32, (tm, LANES), 1)
    lo = lane < HEAD_DIM
    first_half = (lane & (HEAD_DIM // 2)) == 0

    def proc(x, g):
        if do_norm:
            ss = x * x
            s_lo = jnp.sum(jnp.where(lo, ss, 0.0), axis=1, keepdims=True)
            s_hi = jnp.sum(jnp.where(lo, 0.0, ss), axis=1, keepdims=True)
            x = (x * lax.rsqrt(jnp.where(lo, s_lo, s_hi) / HEAD_DIM + EPS)) * g
        if do_rope:
            partner = jnp.where(first_half, pltpu.roll(x, LANES - HEAD_DIM // 2, 1), pltpu.roll(x, HEAD_DIM // 2, 1))
            x = x * cos_ref[...] + partner * sin_ref[...]
        return x

    for p in range(q_ref.shape[2] // LANES):
        sl = slice(p * LANES, (p + 1) * LANES)
        qo_ref[0, :, sl] = (proc(q_ref[0, :, sl].astype(F32), gq_ref[...]) * q_scale).astype(BF16)
    ko_ref[0] = proc(k_ref[0].astype(F32), gk_ref[...]).astype(BF16)


def qk_prep(p3, col_q, col_k, gq, gk, cos, sin, do_norm, do_rope, q_scale, col_v=None):
    B, T, _ = p3.shape
    tm = min(GATTN_TK, T)
    QW = N_QHEADS * HEAD_DIM
    fixed = lambda b, i: (0, 0)
    tile = lambda col: pl.BlockSpec((1, tm, LANES), lambda b, i: (b, i, col // LANES))
    with_vt = col_v is not None
    in_specs = [pl.BlockSpec((1, tm, QW), lambda b, i: (b, i, col_q // QW)), tile(col_k)]
    args = [p3, p3]
    out_specs = [pl.BlockSpec((1, tm, QW), lambda b, i: (b, i, 0)), pl.BlockSpec((1, tm, LANES), lambda b, i: (b, i, 0))]
    out_shape = [jax.ShapeDtypeStruct((B, T, QW), BF16), jax.ShapeDtypeStruct((B, T, LANES), BF16)]
    if with_vt:
        in_specs.append(tile(col_v))
        args.append(p3)
        out_specs.append(pl.BlockSpec((1, 1, VT_ROWS, tm), lambda b, i: (b, i, 0, 0)))
        out_shape.append(jax.ShapeDtypeStruct((B, T // tm, VT_ROWS, tm), BF16))
    in_specs += [pl.BlockSpec((1, LANES), fixed), pl.BlockSpec((1, LANES), fixed),
                 pl.BlockSpec((tm, LANES), lambda b, i: (i, 0)), pl.BlockSpec((tm, LANES), lambda b, i: (i, 0))]
    args += [gq, gk, cos, sin]
    return pl.pallas_call(
        functools.partial(_qkprep_kernel, do_norm=do_norm, do_rope=do_rope, q_scale=q_scale, with_vt=with_vt),
        grid=(B, T // tm),
        in_specs=in_specs, out_specs=out_specs, out_shape=out_shape,
        compiler_params=_cparams(("arbitrary", "arbitrary")),
        name="qk_prep",
    )(*args)


def _stack_pair(q):
    lo = _lo_mask(q.shape)
    zero = jnp.zeros_like(q)
    return jnp.concatenate([jnp.where(lo, q, zero), jnp.where(lo, zero, q)], axis=0)


def _unstack_pair(o, tq):
    return jnp.where(_lo_mask((tq, LANES)), o[:tq], o[tq:])


def _nt_dot(a, b):
    return lax.dot_general(a, b, (((1,), (1,)), ((), ())), preferred_element_type=F32)


GATTN_TK = 512
LOG2E = 1.4426950408889634


def _gattn_kernel(q_ref, k_ref, vt_ref, kc_ref, vtc_ref, o_ref):
    tq = q_ref.shape[1]
    n_pairs = q_ref.shape[2] // LANES
    qqs = [_stack_pair(q_ref[0, :, p * LANES:(p + 1) * LANES]) for p in range(n_pairs)]

    def scores(kb):
        return tuple(_nt_dot(kb, qq) for qq in qqs)

    def update(carries, ss, vtb):
        out = []
        for s, (m, acc) in zip(ss, carries):
            m_new = jnp.maximum(m, jnp.max(s, axis=0, keepdims=True))
            alpha = jnp.exp2(m - m_new)
            p = jnp.exp2((s - m_new).astype(BF16))
            out.append((m_new, alpha * acc + jnp.dot(vtb, p, preferred_element_type=F32)))
        return tuple(out)

    def k_chunk(i):
        return k_ref[0, pl.ds(pl.multiple_of(i * GATTN_TK, GATTN_TK), GATTN_TK), :]

    n_chunks = vt_ref.shape[1]
    carries = tuple((jnp.full((1, 2 * tq), NEG, F32), jnp.zeros((VT_ROWS, 2 * tq), F32)) for _ in range(n_pairs))
    s_first = scores(k_chunk(0))
    carries = update(carries, scores(kc_ref[0]), vtc_ref[0, 0])

    def body(i, state):
        ss, carries = state
        s_next = scores(k_chunk(i + 1))
        return s_next, update(carries, ss, vt_ref[0, i])

    ss, carries = lax.fori_loop(0, n_chunks - 1, body, (s_first, carries))
    carries = update(carries, ss, vt_ref[0, n_chunks - 1])
    for p, (m, acc) in enumerate(carries):
        o_t = acc[0:LANES, :] / acc[LANES:LANES + 1, :]
        o_ref[0, :, p * LANES:(p + 1) * LANES] = _unstack_pair(o_t.T, tq).astype(BF16)


def global_attention(q, k, vt, kc, vtc):
    B, S, QW = q.shape
    C = kc.shape[1]
    tq = 256
    nck = vt.shape[1]
    return pl.pallas_call(
        _gattn_kernel,
        grid=(B, S // tq),
        in_specs=[pl.BlockSpec((1, tq, QW), lambda b, i: (b, i, 0)),
                  pl.BlockSpec((1, S, LANES), lambda b, i: (b, 0, 0)),
                  pl.BlockSpec((1, nck, VT_ROWS, GATTN_TK), lambda b, i: (b, 0, 0, 0)),
                  pl.BlockSpec((1, C, LANES), lambda b, i: (b, 0, 0)),
                  pl.BlockSpec((1, 1, VT_ROWS, C), lambda b, i: (b, 0, 0, 0))],
        out_specs=pl.BlockSpec((1, tq, QW), lambda b, i: (b, i, 0)),
        out_shape=jax.ShapeDtypeStruct((B, S, QW), BF16),
        compiler_params=_cparams(("arbitrary", "arbitrary")),
        name="global_attention",
    )(q, k, vt, kc, vtc)


def _wattn_kernel(sink_ref, q_ref, k_ref, v_ref, kc_ref, vc_ref, o_ref):
    tq = q_ref.shape[1]
    S = k_ref.shape[1]
    span = tq + 2 * WINDOW
    i = pl.program_id(1)
    start = pl.multiple_of(jnp.clip(i * tq - WINDOW, 0, S - span), WINDOW)
    kb = k_ref[0, pl.ds(start, span), :]
    vb = v_ref[0, pl.ds(start, span), :]
    kc = kc_ref[0]
    vc = vc_ref[0]
    qpos = i * tq + lax.broadcasted_iota(jnp.int32, (tq, span), 0)
    kpos = start + lax.broadcasted_iota(jnp.int32, (tq, span), 1)
    valid = jnp.abs(kpos - qpos) <= WINDOW
    valid2 = jnp.concatenate([valid, valid], axis=0)
    top = lax.broadcasted_iota(jnp.int32, (2 * tq, 1), 0) < tq
    for p in range(q_ref.shape[2] // LANES):
        sl = slice(p * LANES, (p + 1) * LANES)
        qq = _stack_pair(q_ref[0, :, sl])
        sink = jnp.where(top, sink_ref[PAIR_ORDER[2 * p]], sink_ref[PAIR_ORDER[2 * p + 1]])
        s_loc = jnp.where(valid2, _nt_dot(qq, kb), NEG)
        s_ctx = _nt_dot(qq, kc)
        m = jnp.maximum(jnp.maximum(jnp.max(s_loc, axis=1, keepdims=True), jnp.max(s_ctx, axis=1, keepdims=True)), sink)
        p_loc = jnp.exp(s_loc - m)
        p_ctx = jnp.exp(s_ctx - m)
        l = jnp.sum(p_loc, axis=1, keepdims=True) + jnp.sum(p_ctx, axis=1, keepdims=True) + jnp.exp(sink - m)
        o = (jnp.dot(p_loc.astype(BF16), vb, preferred_element_type=F32)
             + jnp.dot(p_ctx.astype(BF16), vc, preferred_element_type=F32)) / l
        o_ref[0, :, sl] = _unstack_pair(o, tq).astype(BF16)


def window_attention(sink, q, k, p3, p3c):
    B, S, QW = q.shape
    C = p3c.shape[1]
    tq = 256
    im = lambda j: (lambda b, i: (b, 0, j))
    return pl.pallas_call(
        _wattn_kernel,
        grid=(B, S // tq),
        in_specs=[pl.BlockSpec(memory_space=pltpu.SMEM),
                  pl.BlockSpec((1, tq, QW), lambda b, i: (b, i, 0)),
                  pl.BlockSpec((1, S, LANES), im(0)),
                  pl.BlockSpec((1, S, LANES), im(COL_WV // LANES)),
                  pl.BlockSpec((1, C, LANES), im(COL_WK // LANES)),
                  pl.BlockSpec((1, C, LANES), im(COL_WV // LANES))],
        out_specs=pl.BlockSpec((1, tq, QW), lambda b, i: (b, i, 0)),
        out_shape=jax.ShapeDtypeStruct((B, S, QW), BF16),
        compiler_params=_cparams(("arbitrary", "arbitrary")),
        name="window_attention",
    )(sink, q, k, p3, p3c, p3c)


NB_QROWS = 4
NB_KROWS = NB_QROWS + NB_ROWS


def _nattn_kernel(q_ref, k_ref, v_ref, kc_ref, vc_ref, tab_ref, o_ref):
    tq = q_ref.shape[1]
    rows = k_ref.shape[1] // GRID_W
    nk = NB_KROWS * GRID_W
    j = pl.program_id(1)
    start = pl.multiple_of(jnp.clip(NB_QROWS * j - NB_ROWS // 2, 0, rows - NB_KROWS) * GRID_W, GRID_W)
    for p in range(q_ref.shape[2] // LANES):
        sl = slice(p * LANES, (p + 1) * LANES)
        kb = k_ref[0, pl.ds(start, nk), sl]
        vb = v_ref[0, pl.ds(start, nk), sl]
        kc = kc_ref[0, :, sl]
        vc = vc_ref[0, :, sl]
        qq = _stack_pair(q_ref[0, :, sl] * (HEAD_DIM ** -0.5))
        s_loc = _nt_dot(qq, kb) + tab_ref[0, 2 * p:2 * p + 2].reshape(2 * tq, nk)
        s_ctx = _nt_dot(qq, kc)
        m = jnp.maximum(jnp.max(s_loc, axis=1, keepdims=True), jnp.max(s_ctx, axis=1, keepdims=True))
        p_loc = jnp.exp(s_loc - m)
        p_ctx = jnp.exp(s_ctx - m)
        l = jnp.sum(p_loc, axis=1, keepdims=True) + jnp.sum(p_ctx, axis=1, keepdims=True)
        o = (jnp.dot(p_loc.astype(BF16), vb, preferred_element_type=F32)
             + jnp.dot(p_ctx.astype(BF16), vc, preferred_element_type=F32)) / l
        o_ref[0, :, sl] = _unstack_pair(o, tq).astype(BF16)


def neighbourhood_bias_table(rel_bias, rows):
    nblk = rows // NB_QROWS
    tabs = []
    for jrep in (0, 1, nblk - 1):
        kstart = int(np.clip(NB_QROWS * jrep - NB_ROWS // 2, 0, rows - NB_KROWS))
        ql = np.arange(NB_QROWS * GRID_W)
        kl = np.arange(NB_KROWS * GRID_W)
        r = (NB_QROWS * jrep + ql // GRID_W)[:, None]
        c = (ql % GRID_W)[:, None]
        kr = (kstart + kl // GRID_W)[None, :]
        kc = (kl % GRID_W)[None, :]
        r0 = np.clip(r - NB_ROWS // 2, 0, rows - NB_ROWS)
        c0 = np.clip(c - NB_COLS // 2, 0, GRID_W - NB_COLS)
        valid = (kr >= r0) & (kr < r0 + NB_ROWS) & (kc >= c0) & (kc < c0 + NB_COLS)
        di = np.clip(kr - r + NB_ROWS - 1, 0, 2 * NB_ROWS - 2)
        dj = np.clip(kc - c + NB_COLS - 1, 0, 2 * NB_COLS - 2)
        tabs.append(jnp.where(valid[None], rel_bias[:, di, dj].astype(F32), NEG))
    return jnp.stack(tabs)


def neighbourhood_attention(p3, p3c, table):
    B, S, _ = p3.shape
    C = p3c.shape[1]
    QW = N_QHEADS * HEAD_DIM
    tq = NB_QROWS * GRID_W
    nblk = S // tq
    nk = NB_KROWS * GRID_W
    cls = lambda j: jnp.where(j == 0, 0, jnp.where(j == nblk - 1, 2, 1))
    im = lambda col: (lambda b, j: (b, 0, col // QW))
    return pl.pallas_call(
        _nattn_kernel,
        grid=(B, nblk),
        in_specs=[pl.BlockSpec((1, tq, QW), lambda b, j: (b, j, COL_NQ // QW)),
                  pl.BlockSpec((1, S, QW), im(COL_NK)), pl.BlockSpec((1, S, QW), im(COL_NV)),
                  pl.BlockSpec((1, C, QW), im(COL_NK)), pl.BlockSpec((1, C, QW), im(COL_NV)),
                  pl.BlockSpec((1, N_QHEADS, tq, nk), lambda b, j: (cls(j), 0, 0, 0))],
        out_specs=pl.BlockSpec((1, tq, QW), lambda b, j: (b, j, 0)),
        out_shape=jax.ShapeDtypeStruct((B, S, QW), BF16),
        compiler_params=_cparams(("arbitrary", "arbitrary")),
        name="neighbourhood_attention",
    )(p3, p3, p3, p3c, p3c, table)


def _cattn_kernel(sink_ref, q_ref, k_ref, v_ref, o_ref, *, kv_tiles, use_sink, q_scale):
    C = q_ref.shape[1]
    top = lax.broadcasted_iota(jnp.int32, (2 * C, 1), 0) < C
    for p in range(q_ref.shape[2] // LANES):
        sl = slice(p * LANES, (p + 1) * LANES)
        ksl = sl if kv_tiles > 1 else slice(0, LANES)
        q = q_ref[0, :, sl]
        if q_scale != 1.0:
            q = q * q_scale
        s = _nt_dot(_stack_pair(q), k_ref[0, :, ksl])
        m = jnp.max(s, axis=1, keepdims=True)
        if use_sink:
            sink = jnp.where(top, sink_ref[PAIR_ORDER[2 * p]], sink_ref[PAIR_ORDER[2 * p + 1]])
            m = jnp.maximum(m, sink)
        e = jnp.exp(s - m)
        l = jnp.sum(e, axis=1, keepdims=True)
        if use_sink:
            l = l + jnp.exp(sink - m)
        o = jnp.dot(e.astype(BF16), v_ref[0, :, ksl], preferred_element_type=F32) / l
        o_ref[0, :, sl] = _unstack_pair(o, C).astype(BF16)


def context_attention(sink, q_arr, q_col, k_arr, k_col, v_arr, v_col, kv_tiles, use_sink, q_scale):
    B, C, _ = q_arr.shape
    QW = N_QHEADS * HEAD_DIM
    KW = kv_tiles * LANES
    return pl.pallas_call(
        functools.partial(_cattn_kernel, kv_tiles=kv_tiles, use_sink=use_sink, q_scale=q_scale),
        grid=(B,),
        in_specs=[pl.BlockSpec(memory_space=pltpu.SMEM),
                  pl.BlockSpec((1, C, QW), lambda b: (b, 0, q_col // QW)),
                  pl.BlockSpec((1, C, KW), lambda b: (b, 0, k_col // KW)),
                  pl.BlockSpec((1, C, KW), lambda b: (b, 0, v_col // KW))],
        out_specs=pl.BlockSpec((1, C, QW), lambda b: (b, 0, 0)),
        out_shape=jax.ShapeDtypeStruct((B, C, QW), BF16),
        compiler_params=_cparams(("arbitrary",)),
        name="context_attention",
    )(sink, q_arr, k_arr, v_arr)


def _merge_kernel(h0_ref, h1_ref, mo_ref, ng_ref, ab_ref, ac_ref, ad_ref, g0_ref, g1_ref, g2_ref, g3_ref,
                  wb_ref, wo_ref, x_ref, ga_ref, xo_ref):
    hs = h0_ref[...].astype(F32) + h1_ref[...].astype(F32)
    parts = []
    for h in range(M_HEADS):
        hh = hs[:, h * M_HEAD_DIM:(h + 1) * M_HEAD_DIM]
        parts.append(hh * lax.rsqrt(jnp.mean(hh * hh, axis=1, keepdims=True) + EPS))
    a0 = (jnp.concatenate(parts, axis=1) * ng_ref[...]) * _sigmoid(mo_ref[...].astype(F32))
    branches = (a0.astype(BF16), ab_ref[...], ac_ref[...], ad_ref[...])
    gates = (g0_ref, g1_ref, g2_ref, g3_ref)
    y = None
    for i in range(N_BRANCH):
        t = _sigmoid(gates[i][...].astype(F32)) * jnp.dot(branches[i], wb_ref[i], preferred_element_type=F32)
        y = t if y is None else y + t
    out = jnp.dot(y.astype(BF16), wo_ref[...], preferred_element_type=F32)
    xo_ref[...] = x_ref[...] + ga_ref[0] * out


def merge_branches(h0, h1, p2, norm_g, att_b, att_c, att_d, w_branch, w_out, x, ga, rows_per_mod):
    R, D = x.shape
    tm = min(512, R)
    tpm = rows_per_mod // tm
    row = lambda i: (i, 0)
    fixed = lambda i: (0, 0)
    gate_spec = lambda k: pl.BlockSpec((tm, D), lambda i: (i, COL_GATE // D + k))
    return pl.pallas_call(
        _merge_kernel,
        grid=(R // tm,),
        in_specs=[pl.BlockSpec((tm, M_WIDTH), row), pl.BlockSpec((tm, M_WIDTH), row),
                  pl.BlockSpec((tm, M_WIDTH), lambda i: (i, COL_MO // M_WIDTH)),
                  pl.BlockSpec((1, M_WIDTH), fixed),
                  pl.BlockSpec((tm, BRANCH_WIDTH), row), pl.BlockSpec((tm, BRANCH_WIDTH), row),
                  pl.BlockSpec((tm, BRANCH_WIDTH), row),
                  gate_spec(0), gate_spec(1), gate_spec(2), gate_spec(3),
                  pl.BlockSpec((N_BRANCH, BRANCH_WIDTH, D), lambda i: (0, 0, 0)),
                  pl.BlockSpec((D, D), fixed),
                  pl.BlockSpec((tm, D), row),
                  pl.BlockSpec((1, 1, D), lambda i: (i // tpm, 0, 0))],
        out_specs=pl.BlockSpec((tm, D), row),
        out_shape=jax.ShapeDtypeStruct((R, D), F32),
        compiler_params=_cparams(("arbitrary",)),
        name="merge_branches",
    )(h0, h1, p2, norm_g.reshape(1, -1), att_b, att_c, att_d, p2, p2, p2, p2, w_branch, w_out, x, ga)


def _router_kernel(x_ref, g_ref, sc_ref, sh_ref, wr_ref, rb_ref, h_ref, e_ref, w_ref):
    x = x_ref[...]
    y = x * lax.rsqrt(jnp.mean(x * x, axis=-1, keepdims=True) + EPS)
    h = (y * g_ref[...]) * (1.0 + sc_ref[0]) + sh_ref[0]
    h_ref[...] = h
    logits = lax.dot_general(wr_ref[...], h, (((1,), (1,)), ((), ())), precision=HIGHEST,
                             preferred_element_type=F32)
    scores = _sigmoid(logits)
    sel = scores + rb_ref[...]
    E = EXPERTS_PER_GROUP
    tm = x.shape[0]
    sub = lax.broadcasted_iota(jnp.int32, (E, tm), 0)
    best = None
    for g in range(N_GROUPS):
        v = sel[g * E:(g + 1) * E]
        sc = scores[g * E:(g + 1) * E]
        m1 = jnp.max(v, axis=0, keepdims=True)
        i1 = jnp.min(jnp.where(v == m1, sub, E), axis=0, keepdims=True)
        rest = jnp.where(sub == i1, -jnp.inf, v)
        m2 = jnp.max(rest, axis=0, keepdims=True)
        i2 = jnp.min(jnp.where(rest == m2, sub, E), axis=0, keepdims=True)
        s1 = jnp.sum(jnp.where(sub == i1, sc, 0.0), axis=0, keepdims=True)
        s2 = jnp.sum(jnp.where(sub == i2, sc, 0.0), axis=0, keepdims=True)
        cand = (m1 + m2, g * E + i1, g * E + i2, s1, s2)
        if best is None:
            best = cand
        else:
            take = cand[0] > best[0]
            best = tuple(jnp.where(take, cn, bs) for cn, bs in zip(cand, best))
    _, e1, e2, s1, s2 = best
    tot = s1 + s2
    e_ref[0:1, :] = e1
    e_ref[1:2, :] = e2
    w_ref[0:1, :] = s1 / tot
    w_ref[1:2, :] = s2 / tot


def router(x, g, sc, sh, w_router_t, router_bias, rows_per_mod):
    R, D = x.shape
    tm = min(512, R)
    tpm = rows_per_mod // tm
    mod = lambda i: (i // tpm, 0, 0)
    fixed = lambda i: (0, 0)
    return pl.pallas_call(
        _router_kernel,
        grid=(R // tm,),
        in_specs=[pl.BlockSpec((tm, D), lambda i: (i, 0)), pl.BlockSpec((1, D), fixed),
                  pl.BlockSpec((1, 1, D), mod), pl.BlockSpec((1, 1, D), mod),
                  pl.BlockSpec((N_EXPERTS, D), fixed), pl.BlockSpec((N_EXPERTS, 1), fixed)],
        out_specs=[pl.BlockSpec((tm, D), lambda i: (i, 0)), pl.BlockSpec((2, tm), lambda i: (0, i)),
                   pl.BlockSpec((2, tm), lambda i: (0, i))],
        out_shape=[jax.ShapeDtypeStruct((R, D), F32), jax.ShapeDtypeStruct((2, R), jnp.int32),
                   jax.ShapeDtypeStruct((2, R), F32)],
        compiler_params=_cparams(("arbitrary",)),
        name="router",
    )(x, g.reshape(1, D), sc, sh, w_router_t, router_bias.reshape(-1, 1))


def _expert_kernel(be_ref, nu_ref, tok_ref, nxt_ref, dst_ref, h_hbm, wg_ref, wu_ref, wd_ref, y_hbm,
                   xbuf, ybuf, wgb, wub, wdb, gsem, ssem):
    i = pl.program_id(0)
    n_used = nu_ref[0]
    buf = i % 2
    MB = MOE_BLOCK

    def gather_copy(t, r, b):
        return pltpu.make_async_copy(h_hbm.at[pl.ds(t, 1)], xbuf.at[b, pl.ds(r, 1)], gsem.at[b])

    def scatter_copy(d, r, b):
        return pltpu.make_async_copy(ybuf.at[b, pl.ds(r, 1)], y_hbm.at[pl.ds(d, 1)], ssem.at[b])

    def start_gather(idx_ref, b):
        def body(r, carry):
            gather_copy(idx_ref[0, 0, r], r, b).start()
            return carry
        lax.fori_loop(0, MB, body, 0, unroll=8)

    @pl.when(i == 0)
    def _():
        n_real = y_hbm.shape[0] - 2 * MB
        ybuf[...] = jnp.zeros_like(ybuf)
        for b in range(2):
            fill = pltpu.make_async_copy(ybuf.at[b], y_hbm.at[pl.ds(n_real + b * MB, MB)], ssem.at[b])
            fill.start()
            fill.wait()

    @pl.when(jnp.logical_and(i == 0, n_used > 0))
    def _():
        start_gather(tok_ref, 0)

    @pl.when(i + 1 < n_used)
    def _():
        start_gather(nxt_ref, 1 - buf)

    @pl.when(jnp.logical_and(i >= 2, i - 2 < n_used))
    def _():
        pltpu.make_async_copy(ybuf.at[buf], y_hbm.at[pl.ds(0, MB)], ssem.at[buf]).wait()

    @pl.when(i < n_used)
    def _():
        prev = be_ref[jnp.maximum(i - 1, 0)]

        @pl.when(jnp.logical_or(i == 0, be_ref[i] != prev))
        def _():
            wgb[...] = wg_ref[0].astype(BF16)
            wub[...] = wu_ref[0].astype(BF16)
            wdb[...] = wd_ref[0].astype(BF16)

        pltpu.make_async_copy(h_hbm.at[pl.ds(0, MB)], xbuf.at[buf], gsem.at[buf]).wait()
        x = xbuf[buf].astype(BF16)
        g = jnp.dot(x, wgb[...], preferred_element_type=F32)
        u = jnp.dot(x, wub[...], preferred_element_type=F32)
        hmid = (g * _sigmoid(g)) * u
        ybuf[buf] = jnp.dot(hmid.astype(BF16), wdb[...], preferred_element_type=F32)

        def body(r, carry):
            scatter_copy(dst_ref[0, 0, r], r, buf).start()
            return carry
        lax.fori_loop(0, MB, body, 0, unroll=8)


def expert_ffn(blk_e, n_used, slot_tok, slot_dst, h2, e_gate, e_up, e_down, n_out_rows):
    nblk = slot_tok.shape[0]
    T, D = h2.shape
    last = nblk - 1
    MB = MOE_BLOCK
    wmap = lambda i, be, nu: (be[jnp.minimum(i, last)], 0, 0)
    smem_blk = lambda off: pl.BlockSpec((1, 1, MB), lambda i, be, nu: (jnp.minimum(i + off, last), 0, 0),
                                        memory_space=pltpu.SMEM)
    grid_spec = pltpu.PrefetchScalarGridSpec(
        num_scalar_prefetch=2,
        grid=(nblk + 2,),
        in_specs=[smem_blk(0), smem_blk(1), smem_blk(0),
                  pl.BlockSpec(memory_space=pl.ANY),
                  pl.BlockSpec((1, D, D_EXPERT), wmap), pl.BlockSpec((1, D, D_EXPERT), wmap),
                  pl.BlockSpec((1, D_EXPERT, D), wmap)],
        out_specs=pl.BlockSpec(memory_space=pl.ANY),
        scratch_shapes=[pltpu.VMEM((2, MB, D), F32), pltpu.VMEM((2, MB, D), F32),
                        pltpu.VMEM((D, D_EXPERT), BF16), pltpu.VMEM((D, D_EXPERT), BF16),
                        pltpu.VMEM((D_EXPERT, D), BF16),
                        pltpu.SemaphoreType.DMA((2,)), pltpu.SemaphoreType.DMA((2,))],
    )
    return pl.pallas_call(
        _expert_kernel,
        grid_spec=grid_spec,
        out_shape=jax.ShapeDtypeStruct((n_out_rows, D), F32),
        compiler_params=_cparams(("arbitrary",)),
        name="expert_ffn",
    )(blk_e, n_used, slot_tok, slot_tok, slot_dst, h2, e_gate, e_up, e_down)


def moe_ffn(h2, expert, e_gate, e_up, e_down):
    T, D = h2.shape
    MB = MOE_BLOCK
    n_assign = 2 * T
    e_flat = expert.reshape(-1)
    order = jnp.argsort(e_flat, stable=True).astype(jnp.int32)
    e_s = e_flat[order]
    counts = jnp.bincount(e_flat, length=N_EXPERTS)
    starts = jnp.cumsum(counts) - counts
    padded = ((counts + MB - 1) // MB) * MB
    pends = jnp.cumsum(padded)
    pstarts = pends - padded
    dest = (pstarts[e_s] + (jnp.arange(n_assign) - starts[e_s])).astype(jnp.int32)
    nblk = -(-(n_assign + N_EXPERTS * (MB - 1)) // MB)
    cap = nblk * MB
    slot = jnp.arange(cap, dtype=jnp.int32)
    slot_assign = jnp.full((cap,), -1, jnp.int32).at[dest].set(order)
    dump = n_assign + ((slot // MB) % 2) * MB + slot % MB
    slot_dst = jnp.where(slot_assign >= 0, slot_assign, dump).reshape(nblk, 1, MB)
    slot_tok = jnp.where(slot_assign >= 0, slot_assign % T, 0).reshape(nblk, 1, MB)
    blk_e = jnp.minimum(jnp.searchsorted(pends, jnp.arange(nblk) * MB, side='right'), N_EXPERTS - 1).astype(jnp.int32)
    n_used = (pends[-1] // MB).astype(jnp.int32).reshape(1)
    return expert_ffn(blk_e, n_used, slot_tok, slot_dst, h2, e_gate, e_up, e_down, n_assign + 2 * MB)


def _final_kernel(x_ref, y0_ref, y1_ref, wt_ref, ga_ref, g_ref, o_ref):
    x = x_ref[...] + ga_ref[0] * _moe_combine(y0_ref, y1_ref, wt_ref)
    o_ref[...] = (x * lax.rsqrt(jnp.mean(x * x, axis=-1, keepdims=True) + EPS)) * g_ref[...]


def final_norm(x, moe, ga, g, rows_per_mod):
    R, D = x.shape
    tm = min(1024, R)
    tpm = rows_per_mod // tm
    row = lambda i: (i, 0)
    moe_specs, moe_args = _moe_specs(moe, tm, 1)
    return pl.pallas_call(
        _final_kernel,
        grid=(R // tm,),
        in_specs=[pl.BlockSpec((tm, D), row)] + moe_specs
        + [pl.BlockSpec((1, 1, D), lambda i: (i // tpm, 0, 0)), pl.BlockSpec((1, D), lambda i: (0, 0))],
        out_specs=pl.BlockSpec((tm, D), row),
        out_shape=jax.ShapeDtypeStruct((R, D), F32),
        compiler_params=_cparams(("arbitrary",)),
        name="final_norm",
    )(x, *moe_args, ga, g.reshape(1, D))


_DEINTERLEAVE = np.concatenate([np.arange(0, HEAD_DIM, 2), np.arange(1, HEAD_DIM, 2)])


def _head_cols(n_heads, order, perm):
    return np.concatenate([h * HEAD_DIM + perm for h in order])


def _reorder_w_in(w_in):
    widths = (512, 512, 512, 512, 16, 512, 128, 128, 512, 128, 128, 512, 512, 512, 4096)
    offs = np.concatenate([[0], np.cumsum(widths)])
    (mq, mk, mv, mo, mif, gq, gk, gv, wq, wk, wv, nq, nk, nv, gate) = [
        w_in[:, offs[i]:offs[i + 1]] for i in range(len(widths))]
    qperm = _head_cols(N_QHEADS, PAIR_ORDER, _DEINTERLEAVE)
    kperm = _head_cols(2, (0, 1), _DEINTERLEAVE)
    w = jnp.concatenate([mq, mk, mv, mo, gq[:, qperm], wq[:, qperm], nq, nk, nv,
                         gk[:, kperm], gv, wk[:, kperm], wv, gate], axis=1).astype(BF16)
    wif = jnp.pad(mif, ((0, 0), (0, LANES - mif.shape[1]))).astype(BF16)
    return w, wif


def _rope_tables(S):
    t = np.arange(S)
    n_freq = HEAD_DIM // 4
    inv_freq = jnp.asarray(ROPE_THETA, F32) ** (-jnp.arange(n_freq, dtype=F32) / n_freq)
    row = jnp.asarray(t // GRID_W, F32)
    col = jnp.asarray(t % GRID_W, F32)
    ang = jnp.concatenate([row[:, None] * inv_freq, col[:, None] * inv_freq], axis=-1)
    cos, sin = jnp.cos(ang), jnp.sin(ang)
    cos_t = jnp.tile(jnp.concatenate([cos, cos], axis=1), (1, 2))
    sin_t = jnp.tile(jnp.concatenate([-sin, sin], axis=1), (1, 2))
    return cos_t, sin_t


def kernel(x, c, ctx, c_ctx, w_mod, b_mod, norm1_g, norm2_g, w_in, m_conv_w, m_conv_b, m_gate_b, m_norm_g,
           g_qnorm, g_knorm, w_sink, n_rel_bias, w_branch, w_out, w_router, router_bias, e_gate, e_up, e_down,
           final_g):
    B, S, D = x.shape
    C = ctx.shape[1]
    L = w_mod.shape[0]
    R, RC = B * S, B * C
    rows = S // GRID_W

    c_all = jnp.zeros((16, D), F32).at[:B].set(c).at[B].set(c_ctx)
    mod = modulation(c_all, w_mod, b_mod).reshape(L, 16, 6, D)
    cos_t, sin_t = _rope_tables(S)
    order_rows = _head_cols(N_QHEADS, PAIR_ORDER, np.arange(HEAD_DIM))
    w_router_t = w_router.T

    xl = x.reshape(R, D)
    xc = ctx.reshape(RC, D)
    moe_lat = moe_ctx = ga2_lat = ga2_ctx = None
    for l in range(L):
        need_ctx = l < L - 1
        lat = [mod[l, :B, i].reshape(B, 1, D) for i in range(6)]
        cx = [mod[l, B:B + 1, i].reshape(1, 1, D) for i in range(6)]
        w, wif = _reorder_w_in(w_in[l])
        gb = jnp.pad(m_gate_b[l].reshape(1, -1), ((0, 0), (0, LANES - 16)))
        gq = jnp.tile(g_qnorm[l][_DEINTERLEAVE], 2).reshape(1, LANES)
        gk = jnp.tile(g_knorm[l][_DEINTERLEAVE], 2).reshape(1, LANES)
        wb = w_branch[l].at[1].set(w_branch[l][1][order_rows]).at[2].set(w_branch[l][2][order_rows]).astype(BF16)
        wo = w_out[l].astype(BF16)

        if l == 0:
            p_lat, g_lat = in_projection(xl, norm1_g[l], lat[1], lat[0], w, wif, gb, S)
            p_ctx, g_ctx = in_projection(xc, norm1_g[l], cx[1], cx[0], w, wif, gb, RC)
        else:
            xl, p_lat, g_lat = in_projection(xl, norm1_g[l], lat[1], lat[0], w, wif, gb, S, moe=moe_lat, ga=ga2_lat)
            xc, p_ctx, g_ctx = in_projection(xc, norm1_g[l], cx[1], cx[0], w, wif, gb, RC, moe=moe_ctx, ga=ga2_ctx)
        p3 = p_lat.reshape(B, S, -1)
        p3c = p_ctx.reshape(B, C, -1)

        qk_c = mlstm_conv(p3c, m_conv_w[l], m_conv_b[l])
        qk_l = mlstm_conv(p3, m_conv_w[l], m_conv_b[l])
        c0 = jnp.zeros((B, 2 * M_HEADS, M_HEAD_DIM, 2 * M_HEAD_DIM), F32)
        m0 = jnp.full((B, 2 * M_HEADS, LANES), M_INIT, F32)
        hc0, hc1, c1, m1 = mlstm_scan(qk_c, p3c, g_ctx.reshape(B, C, LANES), c0, m0)
        hl0, hl1, _, _ = mlstm_scan(qk_l, p3, g_lat.reshape(B, S, LANES), c1, m1)

        scale = HEAD_DIM ** -0.5
        gq_l, gk_l, vt_l = qk_prep(p3, COL_GQ, COL_GK, gq, gk, cos_t, sin_t, True, True, scale * LOG2E, col_v=COL_GV)
        gq_c, gk_c, vt_c = qk_prep(p3c, COL_GQ, COL_GK, gq, gk, cos_t[:C], sin_t[:C], True, False, scale, col_v=COL_GV)
        att_b = global_attention(gq_l, gk_l, vt_l, gk_c, vt_c)

        wq_l, wk_l = qk_prep(p3, COL_WQ, COL_WK, gq, gk, cos_t, sin_t, False, True, scale)
        att_c = window_attention(w_sink[l], wq_l, wk_l, p3, p3c)

        table = neighbourhood_bias_table(n_rel_bias[l], rows)
        att_d = neighbourhood_attention(p3, p3c, table)

        xl = merge_branches(hl0.reshape(R, -1), hl1.reshape(R, -1), p_lat, m_norm_g[l], att_b.reshape(R, -1),
                            att_c.reshape(R, -1), att_d.reshape(R, -1), wb, wo, xl, lat[2], S)
        h2, ex, wt = router(xl, norm2_g[l], lat[4], lat[3], w_router_t, router_bias, S)
        if need_ctx:
            cb = context_attention(w_sink[l], gq_c, 0, gk_c, 0, p3c, COL_GV, 1, False, 1.0)
            cc = context_attention(w_sink[l], p3c, COL_WQ, p3c, COL_WK, p3c, COL_WV, 1, True, scale)
            cd = context_attention(w_sink[l], p3c, COL_NQ, p3c, COL_NK, p3c, COL_NV, 4, False, scale)
            xc = merge_branches(hc0.reshape(RC, -1), hc1.reshape(RC, -1), p_ctx, m_norm_g[l], cb.reshape(RC, -1),
                                cc.reshape(RC, -1), cd.reshape(RC, -1), wb, wo, xc, cx[2], RC)
            h2c, exc, wtc = router(xc, norm2_g[l], cx[4], cx[3], w_router_t, router_bias, RC)
            h2 = jnp.concatenate([h2, h2c], axis=0)
            ex = jnp.concatenate([ex, exc], axis=1)
            wt = jnp.concatenate([wt, wtc], axis=1)
        y = moe_ffn(h2, ex, e_gate[l], e_up[l], e_down[l])
        wt_t = wt.T
        moe_lat, ga2_lat = (y, wt_t, 0), lat[5]
        if need_ctx:
            moe_ctx, ga2_ctx = (y, wt_t, R), cx[5]
    out = final_norm(xl, moe_lat, ga2_lat, final_g, S)
    return out.reshape(B, S, D)
```

```python
import functools

import numpy as np
import jax
import jax.numpy as jnp
from jax import lax
from jax.experimental import pallas as pl
from jax.experimental.pallas import tpu as pltpu

F32 = jnp.float32
BF16 = jnp.bfloat16
HIGHEST = lax.Precision.HIGHEST

D_MODEL = 1024
DEPTH = 2
GRID_W = 64
HEAD_DIM = 64
ROPE_THETA = 10000.0
EPS = 1e-6
M_INIT = -1e30
NEG = -1e30
M_HEADS = 4
M_HEAD_DIM = 128
M_WIDTH = M_HEADS * M_HEAD_DIM
M_CHUNK = 128
N_QHEADS = 8
WINDOW = 128
NB_ROWS = 8
NB_COLS = 16
N_BRANCH = 4
BRANCH_WIDTH = 512
N_EXPERTS = 32
N_GROUPS = 4
EXPERTS_PER_GROUP = N_EXPERTS // N_GROUPS
D_EXPERT = 512
MOE_BLOCK = 256
LANES = 128
VMEM_LIMIT = 56 * 1024 * 1024

COL_MQ, COL_MK, COL_MV, COL_MO = 0, 512, 1024, 1536
COL_GQ, COL_WQ, COL_NQ, COL_NK, COL_NV = 2048, 2560, 3072, 3584, 4096
COL_GK, COL_GV, COL_WK, COL_WV = 4608, 4736, 4864, 4992
COL_GATE = 5120
N_PROJ_OUT = COL_GATE + N_BRANCH * D_MODEL
PAIR_ORDER = (0, 4, 1, 5, 2, 6, 3, 7)


def _cparams(sem):
    return pltpu.CompilerParams(dimension_semantics=sem, vmem_limit_bytes=VMEM_LIMIT)


def _sigmoid(x):
    return 1.0 / (1.0 + jnp.exp(-x))


def _lo_mask(shape):
    return lax.broadcasted_iota(jnp.int32, shape, len(shape) - 1) < HEAD_DIM


def _mod_kernel(c_ref, w_ref, b_ref, o_ref):
    c = c_ref[...]
    a = c * _sigmoid(c)
    o_ref[0] = jnp.dot(a, w_ref[0], precision=HIGHEST, preferred_element_type=F32) + b_ref[0]


def modulation(c_all, w_mod, b_mod):
    L, D, N = w_mod.shape
    R = c_all.shape[0]
    tn = 1536
    return pl.pallas_call(
        _mod_kernel,
        grid=(L, N // tn),
        in_specs=[pl.BlockSpec((R, D), lambda l, j: (0, 0)),
                  pl.BlockSpec((1, D, tn), lambda l, j: (l, 0, j)),
                  pl.BlockSpec((1, 1, tn), lambda l, j: (l, 0, j))],
        out_specs=pl.BlockSpec((1, R, tn), lambda l, j: (l, 0, j)),
        out_shape=jax.ShapeDtypeStruct((L, R, N), F32),
        compiler_params=_cparams(("arbitrary", "arbitrary")),
        name="modulation",
    )(c_all, w_mod, b_mod.reshape(L, 1, N))


def _log_sigmoid(z):
    return jnp.minimum(z, 0.0) - jnp.log(1.0 + jnp.exp(-jnp.abs(z)))


def _moe_combine(y0_ref, y1_ref, wt_ref):
    wt = wt_ref[...]
    return wt[:, 0:1] * y0_ref[...] + wt[:, 1:2] * y1_ref[...]


def _inproj_kernel(*refs, has_f):
    if has_f:
        (x_ref, y0_ref, y1_ref, wt_ref, ga_ref, g_ref, sc_ref, sh_ref, w_ref, wif_ref, gb_ref,
         xo_ref, p_ref, gate_ref, h_scr) = refs
    else:
        x_ref, g_ref, sc_ref, sh_ref, w_ref, wif_ref, gb_ref, p_ref, gate_ref, h_scr = refs

    @pl.when(pl.program_id(1) == 0)
    def _():
        x = x_ref[...]
        if has_f:
            x = x + ga_ref[0] * _moe_combine(y0_ref, y1_ref, wt_ref)
            xo_ref[...] = x
        y = x * lax.rsqrt(jnp.mean(x * x, axis=-1, keepdims=True) + EPS)
        h = (y * g_ref[...]) * (1.0 + sc_ref[0]) + sh_ref[0]
        hb = h.astype(BF16)
        h_scr[...] = hb
        z = jnp.dot(hb, wif_ref[...], preferred_element_type=F32) + gb_ref[...]
        lane = lax.broadcasted_iota(jnp.int32, z.shape, 1)
        is_forget = (lane & 0xF4) == 4
        gate_ref[...] = jnp.where(is_forget, _log_sigmoid(z), z)

    p_ref[...] = jnp.dot(h_scr[...], w_ref[...], preferred_element_type=F32).astype(BF16)


def _moe_specs(moe, tm, nidx):
    y, wt, row0 = moe
    T, D = wt.shape[0], y.shape[1]
    assert row0 % tm == 0 and T % tm == 0, (row0, T, tm)
    b0, b1 = row0 // tm, (T + row0) // tm
    if nidx == 2:
        maps = (lambda i, j: (b0 + i, 0), lambda i, j: (b1 + i, 0))
    else:
        maps = (lambda i: (b0 + i, 0), lambda i: (b1 + i, 0))
    return ([pl.BlockSpec((tm, D), maps[0]), pl.BlockSpec((tm, D), maps[1]), pl.BlockSpec((tm, 2), maps[0])],
            [y, y, wt])


def in_projection(x, g, sc, sh, w, wif, gb, rows_per_mod, moe=None, ga=None):
    R, D = x.shape
    N = w.shape[1]
    tm = min(1024, R)
    tn = 1024
    tpm = rows_per_mod // tm
    has_f = moe is not None
    row = lambda i, j: (i, 0)
    mod = lambda i, j: (i // tpm, 0, 0)
    fixed = lambda i, j: (0, 0)
    in_specs = [pl.BlockSpec((tm, D), row)]
    args = [x]
    if has_f:
        moe_specs, moe_args = _moe_specs(moe, tm, 2)
        in_specs += moe_specs + [pl.BlockSpec((1, 1, D), mod)]
        args += moe_args + [ga]
    in_specs += [pl.BlockSpec((1, D), fixed), pl.BlockSpec((1, 1, D), mod), pl.BlockSpec((1, 1, D), mod),
                 pl.BlockSpec((D, tn), lambda i, j: (0, j)), pl.BlockSpec((D, LANES), fixed),
                 pl.BlockSpec((1, LANES), fixed)]
    args += [g.reshape(1, D), sc, sh, w, wif, gb]
    out_specs = [pl.BlockSpec((tm, tn), lambda i, j: (i, j)), pl.BlockSpec((tm, LANES), row)]
    out_shape = [jax.ShapeDtypeStruct((R, N), BF16), jax.ShapeDtypeStruct((R, LANES), F32)]
    if has_f:
        out_specs = [pl.BlockSpec((tm, D), row)] + out_specs
        out_shape = [jax.ShapeDtypeStruct((R, D), F32)] + out_shape
    return pl.pallas_call(
        functools.partial(_inproj_kernel, has_f=has_f),
        grid=(R // tm, N // tn),
        in_specs=in_specs, out_specs=out_specs, out_shape=out_shape,
        scratch_shapes=[pltpu.VMEM((tm, D), BF16)],
        compiler_params=_cparams(("arbitrary", "arbitrary")),
        name="in_projection",
    )(*args)


def _mconv_kernel(x_ref, w_ref, b_ref, o_ref, *, k_scale):
    x = x_ref[0].astype(F32)
    T = x.shape[0]
    row = lax.broadcasted_iota(jnp.int32, x.shape, 0)
    x_prev = jnp.where(row == 0, 0.0, pltpu.roll(x, 1, 0))
    x_next = jnp.where(row == T - 1, 0.0, pltpu.roll(x, T - 1, 0))
    y = x_prev * w_ref[0:1, :] + x * w_ref[1:2, :] + x_next * w_ref[2:3, :] + b_ref[...]
    y = y * _sigmoid(y)
    scale = jnp.where(pl.program_id(1) >= pl.num_programs(1) // 2, k_scale, 1.0)
    o_ref[0] = (y * scale).astype(BF16)


def mlstm_conv(p3, conv_w, conv_b):
    B, T, _ = p3.shape
    tc = 256
    nct = 2 * M_WIDTH // tc
    return pl.pallas_call(
        functools.partial(_mconv_kernel, k_scale=M_HEAD_DIM ** -0.5),
        grid=(B, nct),
        in_specs=[pl.BlockSpec((1, T, tc), lambda b, j: (b, 0, j)),
                  pl.BlockSpec((3, tc), lambda b, j: (0, j)),
                  pl.BlockSpec((1, tc), lambda b, j: (0, j))],
        out_specs=pl.BlockSpec((1, T, tc), lambda b, j: (b, 0, j)),
        out_shape=jax.ShapeDtypeStruct((B, T, 2 * M_WIDTH), BF16),
        compiler_params=_cparams(("arbitrary", "arbitrary")),
        name="mlstm_conv",
    )(p3, conv_w, conv_b.reshape(1, -1))


MLSTM_BT = 2
MV_ROWS = M_HEAD_DIM + 16


def _mlstm_vt_kernel(v_ref, vt_ref):
    ones = jnp.ones((MV_ROWS - M_HEAD_DIM, M_CHUNK), BF16)
    for h in range(M_HEADS):
        vt_ref[0, 0, h, 0:M_HEAD_DIM, :] = v_ref[0, :, h * M_HEAD_DIM:(h + 1) * M_HEAD_DIM].astype(F32).T.astype(BF16)
        vt_ref[0, 0, h, M_HEAD_DIM:MV_ROWS, :] = ones


def mlstm_vt(p3):
    B, T, _ = p3.shape
    nc = T // M_CHUNK
    return pl.pallas_call(
        _mlstm_vt_kernel,
        grid=(B, nc),
        in_specs=[pl.BlockSpec((1, M_CHUNK, M_WIDTH), lambda b, c: (b, c, COL_MV // M_WIDTH))],
        out_specs=pl.BlockSpec((1, 1, M_HEADS, MV_ROWS, M_CHUNK), lambda b, c: (b, c, 0, 0, 0)),
        out_shape=jax.ShapeDtypeStruct((B, nc, M_HEADS, MV_ROWS, M_CHUNK), BF16),
        compiler_params=_cparams(("arbitrary", "arbitrary")),
        name="mlstm_vt",
    )(p3)


def _mlstm_kernel(q0_ref, k0_ref, v0_ref, g0_ref, q1_ref, k1_ref, v1_ref, g1_ref, cin_ref, min_ref,
                  h0_ref, h1_ref, cout_ref, mout_ref, c_scr, m_scr):
    c = pl.program_id(1)
    L = M_CHUNK

    @pl.when(c == 0)
    def _():
        c_scr[...] = cin_ref[...]
        m_scr[...] = min_ref[...]

    row = lax.broadcasted_iota(jnp.int32, (L, L), 0)
    col = lax.broadcasted_iota(jnp.int32, (L, L), 1)
    all_ones = jnp.ones((L, L), F32)
    dirs = ((q0_ref, k0_ref, v0_ref, g0_ref, h0_ref), (q1_ref, k1_ref, v1_ref, g1_ref, h1_ref))
    for bi, (d, (q_ref, k_ref, v_ref, g_ref, h_ref)) in [(bi, dr) for bi in range(q0_ref.shape[0])
                                                         for dr in enumerate(dirs)]:
        G = g_ref[bi]
        tri = jnp.where(col <= row, 1.0, 0.0) if d == 0 else jnp.where(col >= row, 1.0, 0.0)
        CS = jnp.dot(tri, G, precision=HIGHEST, preferred_element_type=F32)
        TOT = jnp.dot(all_ones, G, precision=HIGHEST, preferred_element_type=F32)
        GT, CST, TOTT = G.T, CS.T, TOT.T
        keep = (row <= col) if d == 0 else (row >= col)
        for h in range(M_HEADS):
            ci, cf, sidx = d * 8 + h, d * 8 + 4 + h, d * M_HEADS + h
            b_row, i_row, bend = CST[cf:cf + 1, :], GT[ci:ci + 1, :], TOTT[cf:cf + 1, :]
            col_s = G[:, ci:ci + 1] - CS[:, cf:cf + 1]
            m_prev = m_scr[bi, sidx:sidx + 1, :]
            hs = slice(h * M_HEAD_DIM, (h + 1) * M_HEAD_DIM)
            q = q_ref[bi, :, hs]
            k = k_ref[bi, :, hs]
            vt = v_ref[bi, 0, h]
            ct = c_scr[bi, sidx]

            dmat = jnp.where(keep, b_row + col_s, NEG)
            m_inter = b_row + m_prev
            m_t = jnp.maximum(jnp.max(dmat, axis=0, keepdims=True), m_inter)
            a_t = _nt_dot(k, q) * jnp.exp(dmat - m_t)
            inter = jnp.exp(m_inter - m_t)
            nd = (jnp.dot(vt, a_t.astype(BF16), preferred_element_type=F32)
                  + inter * _nt_dot(ct.astype(BF16), q))
            den = nd[M_HEAD_DIM:M_HEAD_DIM + 1, :]
            h_t = nd[0:M_HEAD_DIM, :] / jnp.maximum(jnp.abs(den), jnp.exp(-m_t))
            h_ref[bi, :, hs] = h_t.T.astype(BF16)

            g_row = bend - b_row + i_row
            m_new = jnp.maximum(bend + m_prev, jnp.max(g_row, axis=1, keepdims=True))
            decay = jnp.exp(bend + m_prev - m_new)
            vw = (vt.astype(F32) * jnp.exp(g_row - m_new)).astype(BF16)
            c_scr[bi, sidx] = decay * ct + jnp.dot(vw, k, preferred_element_type=F32)
            m_scr[bi, sidx:sidx + 1, :] = m_new

    @pl.when(c == pl.num_programs(1) - 1)
    def _():
        cout_ref[...] = c_scr[...]
        mout_ref[...] = m_scr[...]


def mlstm_scan(qk, vt, gates, c_in, m_in):
    B, T, _ = qk.shape
    nc = T // M_CHUNK
    W = M_WIDTH
    bt = MLSTM_BT
    fwd = lambda j: (lambda b, c: (b, c, j))
    bwd = lambda j: (lambda b, c: (b, nc - 1 - c, j))
    st4 = lambda b, c: (b, 0, 0, 0)
    st3 = lambda b, c: (b, 0, 0)
    blk = lambda w, im: pl.BlockSpec((bt, M_CHUNK, w), im)
    vblk = lambda rev: pl.BlockSpec((bt, 1, M_HEADS, MV_ROWS, M_CHUNK),
                                    lambda b, c: (b, nc - 1 - c if rev else c, 0, 0, 0))
    state = pl.BlockSpec((bt, 2 * M_HEADS, MV_ROWS, M_HEAD_DIM), st4)
    in_specs = [blk(W, fwd(0)), blk(W, fwd(1)), vblk(False), blk(LANES, fwd(0)),
                blk(W, bwd(0)), blk(W, bwd(1)), vblk(True), blk(LANES, bwd(0)),
                state, pl.BlockSpec((bt, 2 * M_HEADS, LANES), st3)]
    out_specs = [blk(W, fwd(0)), blk(W, bwd(0)), state, pl.BlockSpec((bt, 2 * M_HEADS, LANES), st3)]
    out_shape = [jax.ShapeDtypeStruct((B, T, W), BF16), jax.ShapeDtypeStruct((B, T, W), BF16),
                 jax.ShapeDtypeStruct(c_in.shape, F32), jax.ShapeDtypeStruct(m_in.shape, F32)]
    return pl.pallas_call(
        _mlstm_kernel,
        grid=(B // bt, nc),
        in_specs=in_specs, out_specs=out_specs, out_shape=out_shape,
        scratch_shapes=[pltpu.VMEM((bt, 2 * M_HEADS, MV_ROWS, M_HEAD_DIM), F32),
                        pltpu.VMEM((bt, 2 * M_HEADS, LANES), F32)],
        compiler_params=_cparams(("arbitrary", "arbitrary")),
        name="mlstm_scan",
    )(qk, qk, vt, gates, qk, qk, vt, gates, c_in, m_in)


VT_ROWS = LANES + 16


def _qkprep_kernel(*refs, do_norm, do_rope, q_scale, with_vt):
    if with_vt:
        q_ref, k_ref, v_ref, gq_ref, gk_ref, cos_ref, sin_ref, qo_ref, ko_ref, vt_ref = refs
        vt_ref[0, 0, 0:LANES, :] = v_ref[0].astype(F32).T.astype(BF16)
        vt_ref[0, 0, LANES:VT_ROWS, :] = jnp.ones((VT_ROWS - LANES, v_ref.shape[1]), BF16)
    else:
        q_ref, k_ref, gq_ref, gk_ref, cos_ref, sin_ref, qo_ref, ko_ref = refs
    tm = q_ref.shape[1]
    lane = lax.broadcasted_iota(jnp.int32, (tm, LANES), 1)
    lo = lane < HEAD_DIM
    first_half = (lane & (HEAD_DIM // 2)) == 0

    def proc(x, g):
        if do_norm:
            ss = x * x
            s_lo = jnp.sum(jnp.where(lo, ss, 0.0), axis=1, keepdims=True)
            s_hi = jnp.sum(jnp.where(lo, 0.0, ss), axis=1, keepdims=True)
            x = (x * lax.rsqrt(jnp.where(lo, s_lo, s_hi) / HEAD_DIM + EPS)) * g
        if do_rope:
            partner = jnp.where(first_half, pltpu.roll(x, LANES - HEAD_DIM // 2, 1), pltpu.roll(x, HEAD_DIM // 2, 1))
            x = x * cos_ref[...] + partner * sin_ref[...]
        return x

    for p in range(q_ref.shape[2] // LANES):
        sl = slice(p * LANES, (p + 1) * LANES)
        qo_ref[0, :, sl] = (proc(q_ref[0, :, sl].astype(F32), gq_ref[...]) * q_scale).astype(BF16)
    ko_ref[0] = proc(k_ref[0].astype(F32), gk_ref[...]).astype(BF16)


def qk_prep(p3, col_q, col_k, gq, gk, cos, sin, do_norm, do_rope, q_scale, col_v=None):
    B, T, _ = p3.shape
    tm = min(GATTN_TK, T)
    QW = N_QHEADS * HEAD_DIM
    fixed = lambda b, i: (0, 0)
    tile = lambda col: pl.BlockSpec((1, tm, LANES), lambda b, i: (b, i, col // LANES))
    with_vt = col_v is not None
    in_specs = [pl.BlockSpec((1, tm, QW), lambda b, i: (b, i, col_q // QW)), tile(col_k)]
    args = [p3, p3]
    out_specs = [pl.BlockSpec((1, tm, QW), lambda b, i: (b, i, 0)), pl.BlockSpec((1, tm, LANES), lambda b, i: (b, i, 0))]
    out_shape = [jax.ShapeDtypeStruct((B, T, QW), BF16), jax.ShapeDtypeStruct((B, T, LANES), BF16)]
    if with_vt:
        in_specs.append(tile(col_v))
        args.append(p3)
        out_specs.append(pl.BlockSpec((1, 1, VT_ROWS, tm), lambda b, i: (b, i, 0, 0)))
        out_shape.append(jax.ShapeDtypeStruct((B, T // tm, VT_ROWS, tm), BF16))
    in_specs += [pl.BlockSpec((1, LANES), fixed), pl.BlockSpec((1, LANES), fixed),
                 pl.BlockSpec((tm, LANES), lambda b, i: (i, 0)), pl.BlockSpec((tm, LANES), lambda b, i: (i, 0))]
    args += [gq, gk, cos, sin]
    return pl.pallas_call(
        functools.partial(_qkprep_kernel, do_norm=do_norm, do_rope=do_rope, q_scale=q_scale, with_vt=with_vt),
        grid=(B, T // tm),
        in_specs=in_specs, out_specs=out_specs, out_shape=out_shape,
        compiler_params=_cparams(("arbitrary", "arbitrary")),
        name="qk_prep",
    )(*args)


def _stack_pair(q):
    lo = _lo_mask(q.shape)
    zero = jnp.zeros_like(q)
    return jnp.concatenate([jnp.where(lo, q, zero), jnp.where(lo, zero, q)], axis=0)


def _unstack_pair(o, tq):
    return jnp.where(_lo_mask((tq, LANES)), o[:tq], o[tq:])


def _nt_dot(a, b):
    return lax.dot_general(a, b, (((1,), (1,)), ((), ())), preferred_element_type=F32)


GATTN_TK = 512
LOG2E = 1.4426950408889634


def _gattn_kernel(q_ref, k_ref, vt_ref, kc_ref, vtc_ref, o_ref):
    tq = q_ref.shape[1]
    n_pairs = q_ref.shape[2] // LANES
    qqs = [_stack_pair(q_ref[0, :, p * LANES:(p + 1) * LANES]) for p in range(n_pairs)]

    def scores(kb):
        return tuple(_nt_dot(kb, qq) for qq in qqs)

    def update(carries, ss, vtb):
        out = []
        for s, (m, acc) in zip(ss, carries):
            m_new = jnp.maximum(m, jnp.max(s, axis=0, keepdims=True))
            alpha = jnp.exp2(m - m_new)
            p = jnp.exp2((s - m_new).astype(BF16))
            out.append((m_new, alpha * acc + jnp.dot(vtb, p, preferred_element_type=F32)))
        return tuple(out)

    def k_chunk(i):
        return k_ref[0, pl.ds(pl.multiple_of(i * GATTN_TK, GATTN_TK), GATTN_TK), :]

    n_chunks = vt_ref.shape[1]
    carries = tuple((jnp.full((1, 2 * tq), NEG, F32), jnp.zeros((VT_ROWS, 2 * tq), F32)) for _ in range(n_pairs))
    s_first = scores(k_chunk(0))
    carries = update(carries, scores(kc_ref[0]), vtc_ref[0, 0])

    def body(i, state):
        ss, carries = state
        s_next = scores(k_chunk(i + 1))
        return s_next, update(carries, ss, vt_ref[0, i])

    ss, carries = lax.fori_loop(0, n_chunks - 1, body, (s_first, carries))
    carries = update(carries, ss, vt_ref[0, n_chunks - 1])
    for p, (m, acc) in enumerate(carries):
        o_t = acc[0:LANES, :] / acc[LANES:LANES + 1, :]
        o_ref[0, :, p * LANES:(p + 1) * LANES] = _unstack_pair(o_t.T, tq).astype(BF16)


def global_attention(q, k, vt, kc, vtc):
    B, S, QW = q.shape
    C = kc.shape[1]
    tq = 256
    nck = vt.shape[1]
    return pl.pallas_call(
        _gattn_kernel,
        grid=(B, S // tq),
        in_specs=[pl.BlockSpec((1, tq, QW), lambda b, i: (b, i, 0)),
                  pl.BlockSpec((1, S, LANES), lambda b, i: (b, 0, 0)),
                  pl.BlockSpec((1, nck, VT_ROWS, GATTN_TK), lambda b, i: (b, 0, 0, 0)),
                  pl.BlockSpec((1, C, LANES), lambda b, i: (b, 0, 0)),
                  pl.BlockSpec((1, 1, VT_ROWS, C), lambda b, i: (b, 0, 0, 0))],
        out_specs=pl.BlockSpec((1, tq, QW), lambda b, i: (b, i, 0)),
        out_shape=jax.ShapeDtypeStruct((B, S, QW), BF16),
        compiler_params=_cparams(("arbitrary", "arbitrary")),
        name="global_attention",
    )(q, k, vt, kc, vtc)


def _wattn_kernel(sink_ref, q_ref, k_ref, v_ref, kc_ref, vc_ref, o_ref):
    tq = q_ref.shape[1]
    S = k_ref.shape[1]
    span = tq + 2 * WINDOW
    i = pl.program_id(1)
    start = pl.multiple_of(jnp.clip(i * tq - WINDOW, 0, S - span), WINDOW)
    kb = k_ref[0, pl.ds(start, span), :]
    vb = v_ref[0, pl.ds(start, span), :]
    kc = kc_ref[0]
    vc = vc_ref[0]
    qpos = i * tq + lax.broadcasted_iota(jnp.int32, (tq, span), 0)
    kpos = start + lax.broadcasted_iota(jnp.int32, (tq, span), 1)
    valid = jnp.abs(kpos - qpos) <= WINDOW
    valid2 = jnp.concatenate([valid, valid], axis=0)
    top = lax.broadcasted_iota(jnp.int32, (2 * tq, 1), 0) < tq
    for p in range(q_ref.shape[2] // LANES):
        sl = slice(p * LANES, (p + 1) * LANES)
        qq = _stack_pair(q_ref[0, :, sl])
        sink = jnp.where(top, sink_ref[PAIR_ORDER[2 * p]], sink_ref[PAIR_ORDER[2 * p + 1]])
        s_loc = jnp.where(valid2, _nt_dot(qq, kb), NEG)
        s_ctx = _nt_dot(qq, kc)
        m = jnp.maximum(jnp.maximum(jnp.max(s_loc, axis=1, keepdims=True), jnp.max(s_ctx, axis=1, keepdims=True)), sink)
        p_loc = jnp.exp(s_loc - m)
        p_ctx = jnp.exp(s_ctx - m)
        l = jnp.sum(p_loc, axis=1, keepdims=True) + jnp.sum(p_ctx, axis=1, keepdims=True) + jnp.exp(sink - m)
        o = (jnp.dot(p_loc.astype(BF16), vb, preferred_element_type=F32)
             + jnp.dot(p_ctx.astype(BF16), vc, preferred_element_type=F32)) / l
        o_ref[0, :, sl] = _unstack_pair(o, tq).astype(BF16)


def window_attention(sink, q, k, p3, p3c):
    B, S, QW = q.shape
    C = p3c.shape[1]
    tq = 256
    im = lambda j: (lambda b, i: (b, 0, j))
    return pl.pallas_call(
        _wattn_kernel,
        grid=(B, S // tq),
        in_specs=[pl.BlockSpec(memory_space=pltpu.SMEM),
                  pl.BlockSpec((1, tq, QW), lambda b, i: (b, i, 0)),
                  pl.BlockSpec((1, S, LANES), im(0)),
                  pl.BlockSpec((1, S, LANES), im(COL_WV // LANES)),
                  pl.BlockSpec((1, C, LANES), im(COL_WK // LANES)),
                  pl.BlockSpec((1, C, LANES), im(COL_WV // LANES))],
        out_specs=pl.BlockSpec((1, tq, QW), lambda b, i: (b, i, 0)),
        out_shape=jax.ShapeDtypeStruct((B, S, QW), BF16),
        compiler_params=_cparams(("arbitrary", "arbitrary")),
        name="window_attention",
    )(sink, q, k, p3, p3c, p3c)


NB_QROWS = 4
NB_KROWS = NB_QROWS + NB_ROWS


def _nattn_kernel(q_ref, k_ref, v_ref, kc_ref, vc_ref, tab_ref, o_ref):
    tq = q_ref.shape[1]
    rows = k_ref.shape[1] // GRID_W
    nk = NB_KROWS * GRID_W
    j = pl.program_id(1)
    start = pl.multiple_of(jnp.clip(NB_QROWS * j - NB_ROWS // 2, 0, rows - NB_KROWS) * GRID_W, GRID_W)
    for p in range(q_ref.shape[2] // LANES):
        sl = slice(p * LANES, (p + 1) * LANES)
        kb = k_ref[0, pl.ds(start, nk), sl]
        vb = v_ref[0, pl.ds(start, nk), sl]
        kc = kc_ref[0, :, sl]
        vc = vc_ref[0, :, sl]
        qq = _stack_pair(q_ref[0, :, sl] * (HEAD_DIM ** -0.5))
        s_loc = _nt_dot(qq, kb) + tab_ref[0, 2 * p:2 * p + 2].reshape(2 * tq, nk)
        s_ctx = _nt_dot(qq, kc)
        m = jnp.maximum(jnp.max(s_loc, axis=1, keepdims=True), jnp.max(s_ctx, axis=1, keepdims=True))
        p_loc = jnp.exp(s_loc - m)
        p_ctx = jnp.exp(s_ctx - m)
        l = jnp.sum(p_loc, axis=1, keepdims=True) + jnp.sum(p_ctx, axis=1, keepdims=True)
        o = (jnp.dot(p_loc.astype(BF16), vb, preferred_element_type=F32)
             + jnp.dot(p_ctx.astype(BF16), vc, preferred_element_type=F32)) / l
        o_ref[0, :, sl] = _unstack_pair(o, tq).astype(BF16)


def neighbourhood_bias_table(rel_bias, rows):
    nblk = rows // NB_QROWS
    H = rel_bias.shape[0]
    W = GRID_W
    pad = W - NB_COLS
    vp = jnp.pad(rel_bias.astype(F32), ((0, 0), (0, 0), (pad, pad)))
    toep = jnp.stack([vp[:, :, W - 1 - qc:2 * W - 1 - qc] for qc in range(W)], axis=2)
    qc = np.arange(W)[:, None]
    kc = np.arange(W)[None, :]
    c0 = np.clip(qc - NB_COLS // 2, 0, W - NB_COLS)
    toep = jnp.where(((kc >= c0) & (kc < c0 + NB_COLS))[None, None], toep, NEG)
    neg_block = jnp.full((H, W, W), NEG, F32)
    tabs = []
    for jrep in (0, 1, nblk - 1):
        kstart = int(np.clip(NB_QROWS * jrep - NB_ROWS // 2, 0, rows - NB_KROWS))
        q_rows = []
        for qr in range(NB_QROWS):
            r = NB_QROWS * jrep + qr
            r0 = int(np.clip(r - NB_ROWS // 2, 0, rows - NB_ROWS))
            blocks = []
            for kr in range(kstart, kstart + NB_KROWS):
                blocks.append(toep[:, kr - r + NB_ROWS - 1] if r0 <= kr < r0 + NB_ROWS else neg_block)
            q_rows.append(jnp.concatenate(blocks, axis=2))
        tabs.append(jnp.concatenate(q_rows, axis=1))
    return jnp.stack(tabs)


def neighbourhood_attention(p3, p3c, table):
    B, S, _ = p3.shape
    C = p3c.shape[1]
    QW = N_QHEADS * HEAD_DIM
    tq = NB_QROWS * GRID_W
    nblk = S // tq
    nk = NB_KROWS * GRID_W
    cls = lambda j: jnp.where(j == 0, 0, jnp.where(j == nblk - 1, 2, 1))
    im = lambda col: (lambda b, j: (b, 0, col // QW))
    return pl.pallas_call(
        _nattn_kernel,
        grid=(B, nblk),
        in_specs=[pl.BlockSpec((1, tq, QW), lambda b, j: (b, j, COL_NQ // QW)),
                  pl.BlockSpec((1, S, QW), im(COL_NK)), pl.BlockSpec((1, S, QW), im(COL_NV)),
                  pl.BlockSpec((1, C, QW), im(COL_NK)), pl.BlockSpec((1, C, QW), im(COL_NV)),
                  pl.BlockSpec((1, N_QHEADS, tq, nk), lambda b, j: (cls(j), 0, 0, 0))],
        out_specs=pl.BlockSpec((1, tq, QW), lambda b, j: (b, j, 0)),
        out_shape=jax.ShapeDtypeStruct((B, S, QW), BF16),
        compiler_params=_cparams(("arbitrary", "arbitrary")),
        name="neighbourhood_attention",
    )(p3, p3, p3, p3c, p3c, table)


def _cattn_kernel(sink_ref, q_ref, k_ref, v_ref, o_ref, *, kv_tiles, use_sink, q_scale):
    C = q_ref.shape[1]
    top = lax.broadcasted_iota(jnp.int32, (2 * C, 1), 0) < C
    for p in range(q_ref.shape[2] // LANES):
        sl = slice(p * LANES, (p + 1) * LANES)
        ksl = sl if kv_tiles > 1 else slice(0, LANES)
        q = q_ref[0, :, sl]
        if q_scale != 1.0:
            q = q * q_scale
        s = _nt_dot(_stack_pair(q), k_ref[0, :, ksl])
        m = jnp.max(s, axis=1, keepdims=True)
        if use_sink:
            sink = jnp.where(top, sink_ref[PAIR_ORDER[2 * p]], sink_ref[PAIR_ORDER[2 * p + 1]])
            m = jnp.maximum(m, sink)
        e = jnp.exp(s - m)
        l = jnp.sum(e, axis=1, keepdims=True)
        if use_sink:
            l = l + jnp.exp(sink - m)
        o = jnp.dot(e.astype(BF16), v_ref[0, :, ksl], preferred_element_type=F32) / l
        o_ref[0, :, sl] = _unstack_pair(o, C).astype(BF16)


def context_attention(sink, q_arr, q_col, k_arr, k_col, v_arr, v_col, kv_tiles, use_sink, q_scale):
    B, C, _ = q_arr.shape
    QW = N_QHEADS * HEAD_DIM
    KW = kv_tiles * LANES
    return pl.pallas_call(
        functools.partial(_cattn_kernel, kv_tiles=kv_tiles, use_sink=use_sink, q_scale=q_scale),
        grid=(B,),
        in_specs=[pl.BlockSpec(memory_space=pltpu.SMEM),
                  pl.BlockSpec((1, C, QW), lambda b: (b, 0, q_col // QW)),
                  pl.BlockSpec((1, C, KW), lambda b: (b, 0, k_col // KW)),
                  pl.BlockSpec((1, C, KW), lambda b: (b, 0, v_col // KW))],
        out_specs=pl.BlockSpec((1, C, QW), lambda b: (b, 0, 0)),
        out_shape=jax.ShapeDtypeStruct((B, C, QW), BF16),
        compiler_params=_cparams(("arbitrary",)),
        name="context_attention",
    )(sink, q_arr, k_arr, v_arr)


def _merge_kernel(h0_ref, h1_ref, mo_ref, ng_ref, ab_ref, ac_ref, ad_ref, g0_ref, g1_ref, g2_ref, g3_ref,
                  wb_ref, wo_ref, x_ref, ga_ref, xo_ref):
    hs = h0_ref[...].astype(F32) + h1_ref[...].astype(F32)
    parts = []
    for h in range(M_HEADS):
        hh = hs[:, h * M_HEAD_DIM:(h + 1) * M_HEAD_DIM]
        parts.append(hh * lax.rsqrt(jnp.mean(hh * hh, axis=1, keepdims=True) + EPS))
    a0 = (jnp.concatenate(parts, axis=1) * ng_ref[...]) * _sigmoid(mo_ref[...].astype(F32))
    branches = (a0.astype(BF16), ab_ref[...], ac_ref[...], ad_ref[...])
    gates = (g0_ref, g1_ref, g2_ref, g3_ref)
    y = None
    for i in range(N_BRANCH):
        t = _sigmoid(gates[i][...].astype(F32)) * jnp.dot(branches[i], wb_ref[i], preferred_element_type=F32)
        y = t if y is None else y + t
    out = jnp.dot(y.astype(BF16), wo_ref[...], preferred_element_type=F32)
    xo_ref[...] = x_ref[...] + ga_ref[0] * out


def merge_branches(h0, h1, p2, norm_g, att_b, att_c, att_d, w_branch, w_out, x, ga, rows_per_mod):
    R, D = x.shape
    tm = min(512, R)
    tpm = rows_per_mod // tm
    row = lambda i: (i, 0)
    fixed = lambda i: (0, 0)
    gate_spec = lambda k: pl.BlockSpec((tm, D), lambda i: (i, COL_GATE // D + k))
    return pl.pallas_call(
        _merge_kernel,
        grid=(R // tm,),
        in_specs=[pl.BlockSpec((tm, M_WIDTH), row), pl.BlockSpec((tm, M_WIDTH), row),
                  pl.BlockSpec((tm, M_WIDTH), lambda i: (i, COL_MO // M_WIDTH)),
                  pl.BlockSpec((1, M_WIDTH), fixed),
                  pl.BlockSpec((tm, BRANCH_WIDTH), row), pl.BlockSpec((tm, BRANCH_WIDTH), row),
                  pl.BlockSpec((tm, BRANCH_WIDTH), row),
                  gate_spec(0), gate_spec(1), gate_spec(2), gate_spec(3),
                  pl.BlockSpec((N_BRANCH, BRANCH_WIDTH, D), lambda i: (0, 0, 0)),
                  pl.BlockSpec((D, D), fixed),
                  pl.BlockSpec((tm, D), row),
                  pl.BlockSpec((1, 1, D), lambda i: (i // tpm, 0, 0))],
        out_specs=pl.BlockSpec((tm, D), row),
        out_shape=jax.ShapeDtypeStruct((R, D), F32),
        compiler_params=_cparams(("arbitrary",)),
        name="merge_branches",
    )(h0, h1, p2, norm_g.reshape(1, -1), att_b, att_c, att_d, p2, p2, p2, p2, w_branch, w_out, x, ga)


def _router_kernel(x_ref, g_ref, sc_ref, sh_ref, wr_ref, rb_ref, h_ref, e_ref, w_ref):
    x = x_ref[...]
    y = x * lax.rsqrt(jnp.mean(x * x, axis=-1, keepdims=True) + EPS)
    h = (y * g_ref[...]) * (1.0 + sc_ref[0]) + sh_ref[0]
    h_ref[...] = h
    logits = lax.dot_general(wr_ref[...], h, (((1,), (1,)), ((), ())), precision=HIGHEST,
                             preferred_element_type=F32)
    scores = _sigmoid(logits)
    sel = scores + rb_ref[...]
    E = EXPERTS_PER_GROUP
    tm = x.shape[0]
    sub = lax.broadcasted_iota(jnp.int32, (E, tm), 0)
    best = None
    for g in range(N_GROUPS):
        v = sel[g * E:(g + 1) * E]
        sc = scores[g * E:(g + 1) * E]
        m1 = jnp.max(v, axis=0, keepdims=True)
        i1 = jnp.min(jnp.where(v == m1, sub, E), axis=0, keepdims=True)
        rest = jnp.where(sub == i1, -jnp.inf, v)
        m2 = jnp.max(rest, axis=0, keepdims=True)
        i2 = jnp.min(jnp.where(rest == m2, sub, E), axis=0, keepdims=True)
        s1 = jnp.sum(jnp.where(sub == i1, sc, 0.0), axis=0, keepdims=True)
        s2 = jnp.sum(jnp.where(sub == i2, sc, 0.0), axis=0, keepdims=True)
        cand = (m1 + m2, g * E + i1, g * E + i2, s1, s2)
        if best is None:
            best = cand
        else:
            take = cand[0] > best[0]
            best = tuple(jnp.where(take, cn, bs) for cn, bs in zip(cand, best))
    _, e1, e2, s1, s2 = best
    tot = s1 + s2
    e_ref[0:1, :] = e1
    e_ref[1:2, :] = e2
    w_ref[0:1, :] = s1 / tot
    w_ref[1:2, :] = s2 / tot


def router(x, g, sc, sh, w_router_t, router_bias, rows_per_mod):
    R, D = x.shape
    tm = min(512, R)
    tpm = rows_per_mod // tm
    mod = lambda i: (i // tpm, 0, 0)
    fixed = lambda i: (0, 0)
    return pl.pallas_call(
        _router_kernel,
        grid=(R // tm,),
        in_specs=[pl.BlockSpec((tm, D), lambda i: (i, 0)), pl.BlockSpec((1, D), fixed),
                  pl.BlockSpec((1, 1, D), mod), pl.BlockSpec((1, 1, D), mod),
                  pl.BlockSpec((N_EXPERTS, D), fixed), pl.BlockSpec((N_EXPERTS, 1), fixed)],
        out_specs=[pl.BlockSpec((tm, D), lambda i: (i, 0)), pl.BlockSpec((2, tm), lambda i: (0, i)),
                   pl.BlockSpec((2, tm), lambda i: (0, i))],
        out_shape=[jax.ShapeDtypeStruct((R, D), F32), jax.ShapeDtypeStruct((2, R), jnp.int32),
                   jax.ShapeDtypeStruct((2, R), F32)],
        compiler_params=_cparams(("arbitrary",)),
        name="router",
    )(x, g.reshape(1, D), sc, sh, w_router_t, router_bias.reshape(-1, 1))


def _expert_kernel(be_ref, nu_ref, tok_ref, nxt_ref, dst_ref, h_hbm, wg_ref, wu_ref, wd_ref, y_hbm,
                   xbuf, ybuf, wgb, wub, wdb, gsem, ssem):
    i = pl.program_id(0)
    n_used = nu_ref[0]
    buf = i % 2
    MB = MOE_BLOCK

    def gather_copy(t, r, b):
        return pltpu.make_async_copy(h_hbm.at[pl.ds(t, 1)], xbuf.at[b, pl.ds(r, 1)], gsem.at[b])

    def scatter_copy(d, r, b):
        return pltpu.make_async_copy(ybuf.at[b, pl.ds(r, 1)], y_hbm.at[pl.ds(d, 1)], ssem.at[b])

    def start_gather(idx_ref, b):
        for r in range(MB):
            gather_copy(idx_ref[0, 0, r], r, b).start(priority=r % 2)

    @pl.when(i == 0)
    def _():
        n_real = y_hbm.shape[0] - 2 * MB
        ybuf[...] = jnp.zeros_like(ybuf)
        for b in range(2):
            fill = pltpu.make_async_copy(ybuf.at[b], y_hbm.at[pl.ds(n_real + b * MB, MB)], ssem.at[b])
            fill.start()
            fill.wait()

    @pl.when(jnp.logical_and(i == 0, n_used > 0))
    def _():
        start_gather(tok_ref, 0)

    @pl.when(i + 1 < n_used)
    def _():
        start_gather(nxt_ref, 1 - buf)

    @pl.when(jnp.logical_and(i >= 2, i - 2 < n_used))
    def _():
        pltpu.make_async_copy(ybuf.at[buf], y_hbm.at[pl.ds(0, MB)], ssem.at[buf]).wait()

    @pl.when(i < n_used)
    def _():
        prev = be_ref[jnp.maximum(i - 1, 0)]

        @pl.when(jnp.logical_or(i == 0, be_ref[i] != prev))
        def _():
            wgb[...] = wg_ref[0, 0].astype(BF16)
            wub[...] = wu_ref[0, 0].astype(BF16)
            wdb[...] = wd_ref[0, 0].astype(BF16)

        pltpu.make_async_copy(h_hbm.at[pl.ds(0, MB)], xbuf.at[buf], gsem.at[buf]).wait()
        x = xbuf[buf].astype(BF16)
        g = jnp.dot(x, wgb[...], preferred_element_type=F32)
        u = jnp.dot(x, wub[...], preferred_element_type=F32)
        hmid = (g * _sigmoid(g)) * u
        ybuf[buf] = jnp.dot(hmid.astype(BF16), wdb[...], preferred_element_type=F32)

        for r in range(MB):
            scatter_copy(dst_ref[0, 0, r], r, buf).start(priority=r % 2)


def expert_ffn(blk_e, n_used, slot_tok, slot_dst, h2, e_gate, e_up, e_down, layer, n_out_rows):
    nblk = slot_tok.shape[0]
    T, D = h2.shape
    last = nblk - 1
    MB = MOE_BLOCK
    wmap = lambda i, be, nu: (layer, be[jnp.minimum(i, last)], 0, 0)
    smem_blk = lambda off: pl.BlockSpec((1, 1, MB), lambda i, be, nu: (jnp.minimum(i + off, last), 0, 0),
                                        memory_space=pltpu.SMEM)
    grid_spec = pltpu.PrefetchScalarGridSpec(
        num_scalar_prefetch=2,
        grid=(nblk + 2,),
        in_specs=[smem_blk(0), smem_blk(1), smem_blk(0),
                  pl.BlockSpec(memory_space=pl.ANY),
                  pl.BlockSpec((1, 1, D, D_EXPERT), wmap), pl.BlockSpec((1, 1, D, D_EXPERT), wmap),
                  pl.BlockSpec((1, 1, D_EXPERT, D), wmap)],
        out_specs=pl.BlockSpec(memory_space=pl.ANY),
        scratch_shapes=[pltpu.VMEM((2, MB, D), F32), pltpu.VMEM((2, MB, D), F32),
                        pltpu.VMEM((D, D_EXPERT), BF16), pltpu.VMEM((D, D_EXPERT), BF16),
                        pltpu.VMEM((D_EXPERT, D), BF16),
                        pltpu.SemaphoreType.DMA((2,)), pltpu.SemaphoreType.DMA((2,))],
    )
    return pl.pallas_call(
        _expert_kernel,
        grid_spec=grid_spec,
        out_shape=jax.ShapeDtypeStruct((n_out_rows, D), F32),
        compiler_params=_cparams(("arbitrary",)),
        name="expert_ffn",
    )(blk_e, n_used, slot_tok, slot_tok, slot_dst, h2, e_gate, e_up, e_down)


def moe_ffn(h2, expert, e_gate, e_up, e_down, layer):
    T, D = h2.shape
    MB = MOE_BLOCK
    n_assign = 2 * T
    e_flat = expert.reshape(-1)
    order = jnp.argsort(e_flat, stable=True).astype(jnp.int32)
    eids = jnp.arange(N_EXPERTS, dtype=jnp.int32)
    counts = jnp.sum((e_flat[:, None] == eids[None, :]).astype(jnp.int32), axis=0)
    starts = jnp.cumsum(counts) - counts
    padded = ((counts + MB - 1) // MB) * MB
    pends = jnp.cumsum(padded)
    pstarts = pends - padded
    nblk = -(-(n_assign + N_EXPERTS * (MB - 1)) // MB)
    cap = nblk * MB
    slot = jnp.arange(cap, dtype=jnp.int32)
    slot_e = jnp.sum((slot[:, None] >= pends[None, :]).astype(jnp.int32), axis=1)
    onehot = slot_e[:, None] == eids[None, :]
    pick = lambda v: jnp.sum(jnp.where(onehot, v[None, :], 0), axis=1)
    local = slot - pick(pstarts)
    real = jnp.logical_and(slot_e < N_EXPERTS, local < pick(counts))
    slot_assign = order[jnp.clip(pick(starts) + local, 0, n_assign - 1)]
    dump = n_assign + ((slot // MB) % 2) * MB + slot % MB
    slot_dst = jnp.where(real, slot_assign, dump).reshape(nblk, 1, MB)
    slot_tok = jnp.where(real, slot_assign % T, 0).reshape(nblk, 1, MB)
    blk_e = jnp.minimum(slot_e.reshape(nblk, MB)[:, 0], N_EXPERTS - 1)
    n_used = (pends[-1] // MB).astype(jnp.int32).reshape(1)
    return expert_ffn(blk_e, n_used, slot_tok, slot_dst, h2, e_gate, e_up, e_down, layer, n_assign + 2 * MB)


def _final_kernel(x_ref, y0_ref, y1_ref, wt_ref, ga_ref, g_ref, o_ref):
    x = x_ref[...] + ga_ref[0] * _moe_combine(y0_ref, y1_ref, wt_ref)
    o_ref[...] = (x * lax.rsqrt(jnp.mean(x * x, axis=-1, keepdims=True) + EPS)) * g_ref[...]


def final_norm(x, moe, ga, g, rows_per_mod):
    R, D = x.shape
    tm = min(1024, R)
    tpm = rows_per_mod // tm
    row = lambda i: (i, 0)
    moe_specs, moe_args = _moe_specs(moe, tm, 1)
    return pl.pallas_call(
        _final_kernel,
        grid=(R // tm,),
        in_specs=[pl.BlockSpec((tm, D), row)] + moe_specs
        + [pl.BlockSpec((1, 1, D), lambda i: (i // tpm, 0, 0)), pl.BlockSpec((1, D), lambda i: (0, 0))],
        out_specs=pl.BlockSpec((tm, D), row),
        out_shape=jax.ShapeDtypeStruct((R, D), F32),
        compiler_params=_cparams(("arbitrary",)),
        name="final_norm",
    )(x, *moe_args, ga, g.reshape(1, D))


_DEINTERLEAVE = np.concatenate([np.arange(0, HEAD_DIM, 2), np.arange(1, HEAD_DIM, 2)])


def _head_cols(n_heads, order, perm):
    return np.concatenate([h * HEAD_DIM + perm for h in order])


def _reorder_w_in(w_in):
    widths = (512, 512, 512, 512, 16, 512, 128, 128, 512, 128, 128, 512, 512, 512, 4096)
    offs = np.concatenate([[0], np.cumsum(widths)])
    (mq, mk, mv, mo, mif, gq, gk, gv, wq, wk, wv, nq, nk, nv, gate) = [
        w_in[:, offs[i]:offs[i + 1]] for i in range(len(widths))]
    D = w_in.shape[0]
    half = HEAD_DIM // 2

    def deinterleave(a, n_heads):
        return a.reshape(D, n_heads, half, 2).transpose(0, 1, 3, 2).reshape(D, n_heads * HEAD_DIM)

    def pair_heads(a):
        return a.reshape(D, 2, N_QHEADS // 2, HEAD_DIM).transpose(0, 2, 1, 3).reshape(D, N_QHEADS * HEAD_DIM)

    w = jnp.concatenate([mq, mk, mv, mo, pair_heads(deinterleave(gq, N_QHEADS)), pair_heads(deinterleave(wq, N_QHEADS)),
                         nq, nk, nv, deinterleave(gk, 2), gv, deinterleave(wk, 2), wv, gate], axis=1).astype(BF16)
    wif = jnp.pad(mif, ((0, 0), (0, LANES - mif.shape[1]))).astype(BF16)
    return w, wif


def _rope_tables(S):
    t = np.arange(S)
    n_freq = HEAD_DIM // 4
    inv_freq = jnp.asarray(ROPE_THETA, F32) ** (-jnp.arange(n_freq, dtype=F32) / n_freq)
    row = jnp.asarray(t // GRID_W, F32)
    col = jnp.asarray(t % GRID_W, F32)
    ang = jnp.concatenate([row[:, None] * inv_freq, col[:, None] * inv_freq], axis=-1)
    cos, sin = jnp.cos(ang), jnp.sin(ang)
    cos_t = jnp.tile(jnp.concatenate([cos, cos], axis=1), (1, 2))
    sin_t = jnp.tile(jnp.concatenate([-sin, sin], axis=1), (1, 2))
    return cos_t, sin_t


def kernel(x, c, ctx, c_ctx, w_mod, b_mod, norm1_g, norm2_g, w_in, m_conv_w, m_conv_b, m_gate_b, m_norm_g,
           g_qnorm, g_knorm, w_sink, n_rel_bias, w_branch, w_out, w_router, router_bias, e_gate, e_up, e_down,
           final_g):
    B, S, D = x.shape
    C = ctx.shape[1]
    L = w_mod.shape[0]
    R, RC = B * S, B * C
    rows = S // GRID_W

    c_all = jnp.zeros((16, D), F32).at[:B].set(c).at[B].set(c_ctx)
    mod = modulation(c_all, w_mod, b_mod).reshape(L, 16, 6, D)
    cos_t, sin_t = _rope_tables(S)
    w_router_t = w_router.T

    xl = x.reshape(R, D)
    xc = ctx.reshape(RC, D)
    moe_lat = moe_ctx = ga2_lat = ga2_ctx = None
    for l in range(L):
        need_ctx = l < L - 1
        lat = [mod[l, :B, i].reshape(B, 1, D) for i in range(6)]
        cx = [mod[l, B:B + 1, i].reshape(1, 1, D) for i in range(6)]
        w, wif = _reorder_w_in(w_in[l])
        gb = jnp.pad(m_gate_b[l].reshape(1, -1), ((0, 0), (0, LANES - 16)))
        gq = jnp.tile(g_qnorm[l][_DEINTERLEAVE], 2).reshape(1, LANES)
        gk = jnp.tile(g_knorm[l][_DEINTERLEAVE], 2).reshape(1, LANES)
        pair_rows = lambda a: a.reshape(2, N_QHEADS // 2, HEAD_DIM, D).transpose(1, 0, 2, 3).reshape(BRANCH_WIDTH, D)
        wb = jnp.stack([w_branch[l, 0], pair_rows(w_branch[l, 1]), pair_rows(w_branch[l, 2]),
                        w_branch[l, 3]]).astype(BF16)
        wo = w_out[l].astype(BF16)

        if l == 0:
            p_lat, g_lat = in_projection(xl, norm1_g[l], lat[1], lat[0], w, wif, gb, S)
            p_ctx, g_ctx = in_projection(xc, norm1_g[l], cx[1], cx[0], w, wif, gb, RC)
        else:
            xl, p_lat, g_lat = in_projection(xl, norm1_g[l], lat[1], lat[0], w, wif, gb, S, moe=moe_lat, ga=ga2_lat)
            xc, p_ctx, g_ctx = in_projection(xc, norm1_g[l], cx[1], cx[0], w, wif, gb, RC, moe=moe_ctx, ga=ga2_ctx)
        p3 = p_lat.reshape(B, S, -1)
        p3c = p_ctx.reshape(B, C, -1)

        qk_c = mlstm_conv(p3c, m_conv_w[l], m_conv_b[l])
        qk_l = mlstm_conv(p3, m_conv_w[l], m_conv_b[l])
        c0 = jnp.zeros((B, 2 * M_HEADS, MV_ROWS, M_HEAD_DIM), F32)
        m0 = jnp.full((B, 2 * M_HEADS, LANES), M_INIT, F32)
        hc0, hc1, c1, m1 = mlstm_scan(qk_c, mlstm_vt(p3c), g_ctx.reshape(B, C, LANES), c0, m0)
        hl0, hl1, _, _ = mlstm_scan(qk_l, mlstm_vt(p3), g_lat.reshape(B, S, LANES), c1, m1)

        scale = HEAD_DIM ** -0.5
        gq_l, gk_l, vt_l = qk_prep(p3, COL_GQ, COL_GK, gq, gk, cos_t, sin_t, True, True, scale * LOG2E, col_v=COL_GV)
        gq_c, gk_c, vt_c = qk_prep(p3c, COL_GQ, COL_GK, gq, gk, cos_t[:C], sin_t[:C], True, False, scale, col_v=COL_GV)
        att_b = global_attention(gq_l, gk_l, vt_l, gk_c, vt_c)

        wq_l, wk_l = qk_prep(p3, COL_WQ, COL_WK, gq, gk, cos_t, sin_t, False, True, scale)
        att_c = window_attention(w_sink[l], wq_l, wk_l, p3, p3c)

        table = neighbourhood_bias_table(n_rel_bias[l], rows)
        att_d = neighbourhood_attention(p3, p3c, table)

        xl = merge_branches(hl0.reshape(R, -1), hl1.reshape(R, -1), p_lat, m_norm_g[l], att_b.reshape(R, -1),
                            att_c.reshape(R, -1), att_d.reshape(R, -1), wb, wo, xl, lat[2], S)
        h2, ex, wt = router(xl, norm2_g[l], lat[4], lat[3], w_router_t, router_bias, S)
        if need_ctx:
            cb = context_attention(w_sink[l], gq_c, 0, gk_c, 0, p3c, COL_GV, 1, False, 1.0)
            cc = context_attention(w_sink[l], p3c, COL_WQ, p3c, COL_WK, p3c, COL_WV, 1, True, scale)
            cd = context_attention(w_sink[l], p3c, COL_NQ, p3c, COL_NK, p3c, COL_NV, 4, False, scale)
            xc = merge_branches(hc0.reshape(RC, -1), hc1.reshape(RC, -1), p_ctx, m_norm_g[l], cb.reshape(RC, -1),
                                cc.reshape(RC, -1), cd.reshape(RC, -1), wb, wo, xc, cx[2], RC)
            h2c, exc, wtc = router(xc, norm2_g[l], cx[4], cx[3], w_router_t, router_bias, RC)
            h2 = jnp.concatenate([h2, h2c], axis=0)
            ex = jnp.concatenate([ex, exc], axis=1)
            wt = jnp.concatenate([wt, wtc], axis=1)
        y = moe_ffn(h2, ex, e_gate, e_up, e_down, l)
        wt_t = wt.T
        moe_lat, ga2_lat = (y, wt_t, 0), lat[5]
        if need_ctx:
            moe_ctx, ga2_ctx = (y, wt_t, R), cx[5]
    out = final_norm(xl, moe_lat, ga2_lat, final_g, S)
    return out.reshape(B, S, D)
```

```python
import functools

import numpy as np
import jax
import jax.numpy as jnp
from jax import lax
from jax.experimental import pallas as pl
from jax.experimental.pallas import tpu as pltpu

F32 = jnp.float32
BF16 = jnp.bfloat16
HIGHEST = lax.Precision.HIGHEST

D_MODEL = 1024
DEPTH = 2
GRID_W = 64
HEAD_DIM = 64
ROPE_THETA = 10000.0
EPS = 1e-6
M_INIT = -1e30
NEG = -1e30
M_HEADS = 4
M_HEAD_DIM = 128
M_WIDTH = M_HEADS * M_HEAD_DIM
M_CHUNK = 128
N_QHEADS = 8
WINDOW = 128
NB_ROWS = 8
NB_COLS = 16
N_BRANCH = 4
BRANCH_WIDTH = 512
N_EXPERTS = 32
N_GROUPS = 4
EXPERTS_PER_GROUP = N_EXPERTS // N_GROUPS
D_EXPERT = 512
MOE_BLOCK = 256
LANES = 128
VMEM_LIMIT = 56 * 1024 * 1024

COL_MQ, COL_MK, COL_MV, COL_MO = 0, 512, 1024, 1536
COL_GQ, COL_WQ, COL_NQ, COL_NK, COL_NV = 2048, 2560, 3072, 3584, 4096
COL_GK, COL_GV, COL_WK, COL_WV = 4608, 4736, 4864, 4992
COL_GATE = 5120
N_PROJ_OUT = COL_GATE + N_BRANCH * D_MODEL
PAIR_ORDER = (0, 4, 1, 5, 2, 6, 3, 7)


def _cparams(sem):
    return pltpu.CompilerParams(dimension_semantics=sem, vmem_limit_bytes=VMEM_LIMIT)


def _sigmoid(x):
    return 1.0 / (1.0 + jnp.exp(-x))


def _lo_mask(shape):
    return lax.broadcasted_iota(jnp.int32, shape, len(shape) - 1) < HEAD_DIM


def _mod_kernel(c_ref, w_ref, b_ref, o_ref):
    c = c_ref[...]
    a = c * _sigmoid(c)
    o_ref[0] = jnp.dot(a, w_ref[0], precision=HIGHEST, preferred_element_type=F32) + b_ref[0]


def modulation(c_all, w_mod, b_mod):
    L, D, N = w_mod.shape
    R = c_all.shape[0]
    tn = 1536
    return pl.pallas_call(
        _mod_kernel,
        grid=(L, N // tn),
        in_specs=[pl.BlockSpec((R, D), lambda l, j: (0, 0)),
                  pl.BlockSpec((1, D, tn), lambda l, j: (l, 0, j)),
                  pl.BlockSpec((1, 1, tn), lambda l, j: (l, 0, j))],
        out_specs=pl.BlockSpec((1, R, tn), lambda l, j: (l, 0, j)),
        out_shape=jax.ShapeDtypeStruct((L, R, N), F32),
        compiler_params=_cparams(("arbitrary", "arbitrary")),
        name="modulation",
    )(c_all, w_mod, b_mod.reshape(L, 1, N))


def _log_sigmoid(z):
    return jnp.minimum(z, 0.0) - jnp.log(1.0 + jnp.exp(-jnp.abs(z)))


ROW_TILE = D_MODEL // LANES


def _to_token_tiles(ref, val, n):
    for j in range(ROW_TILE):
        ref[pl.ds(j, n, stride=ROW_TILE), :] = val[:, j * LANES:(j + 1) * LANES]


def _from_token_tiles(ref, n):
    return jnp.concatenate([ref[pl.ds(j, n, stride=ROW_TILE), :] for j in range(ROW_TILE)], axis=1)


def _moe_combine(y0_ref, y1_ref, wt_ref):
    wt = wt_ref[...]
    n = wt.shape[0]
    return wt[:, 0:1] * _from_token_tiles(y0_ref, n) + wt[:, 1:2] * _from_token_tiles(y1_ref, n)


def _inproj_kernel(*refs, has_f):
    if has_f:
        (x_ref, y0_ref, y1_ref, wt_ref, ga_ref, g_ref, sc_ref, sh_ref, w_ref, wif_ref, gb_ref,
         xo_ref, p_ref, gate_ref, h_scr) = refs
    else:
        x_ref, g_ref, sc_ref, sh_ref, w_ref, wif_ref, gb_ref, p_ref, gate_ref, h_scr = refs

    @pl.when(pl.program_id(1) == 0)
    def _():
        x = x_ref[...]
        if has_f:
            x = x + ga_ref[0] * _moe_combine(y0_ref, y1_ref, wt_ref)
            xo_ref[...] = x
        y = x * lax.rsqrt(jnp.mean(x * x, axis=-1, keepdims=True) + EPS)
        h = (y * g_ref[...]) * (1.0 + sc_ref[0]) + sh_ref[0]
        hb = h.astype(BF16)
        h_scr[...] = hb
        z = jnp.dot(hb, wif_ref[...], preferred_element_type=F32) + gb_ref[...]
        lane = lax.broadcasted_iota(jnp.int32, z.shape, 1)
        is_forget = (lane & 0xF4) == 4
        gate_ref[...] = jnp.where(is_forget, _log_sigmoid(z), z)

    p_ref[...] = jnp.dot(h_scr[...], w_ref[...], preferred_element_type=F32).astype(BF16)


def _moe_specs(moe, tm, nidx):
    y, wt, row0 = moe
    T = wt.shape[0]
    assert row0 % tm == 0 and T % tm == 0, (row0, T, tm)
    b0, b1 = row0 // tm, (T + row0) // tm
    if nidx == 2:
        maps = (lambda i, j: (b0 + i, 0), lambda i, j: (b1 + i, 0))
    else:
        maps = (lambda i: (b0 + i, 0), lambda i: (b1 + i, 0))
    tile_rows = pl.BlockSpec((tm * ROW_TILE, LANES), maps[0]), pl.BlockSpec((tm * ROW_TILE, LANES), maps[1])
    return [*tile_rows, pl.BlockSpec((tm, 2), maps[0])], [y, y, wt]


def in_projection(x, g, sc, sh, w, wif, gb, rows_per_mod, moe=None, ga=None):
    R, D = x.shape
    N = w.shape[1]
    tm = min(1024, R)
    tn = 1024
    tpm = rows_per_mod // tm
    has_f = moe is not None
    row = lambda i, j: (i, 0)
    mod = lambda i, j: (i // tpm, 0, 0)
    fixed = lambda i, j: (0, 0)
    in_specs = [pl.BlockSpec((tm, D), row)]
    args = [x]
    if has_f:
        moe_specs, moe_args = _moe_specs(moe, tm, 2)
        in_specs += moe_specs + [pl.BlockSpec((1, 1, D), mod)]
        args += moe_args + [ga]
    in_specs += [pl.BlockSpec((1, D), fixed), pl.BlockSpec((1, 1, D), mod), pl.BlockSpec((1, 1, D), mod),
                 pl.BlockSpec((D, tn), lambda i, j: (0, j)), pl.BlockSpec((D, LANES), fixed),
                 pl.BlockSpec((1, LANES), fixed)]
    args += [g.reshape(1, D), sc, sh, w, wif, gb]
    out_specs = [pl.BlockSpec((tm, tn), lambda i, j: (i, j)), pl.BlockSpec((tm, LANES), row)]
    out_shape = [jax.ShapeDtypeStruct((R, N), BF16), jax.ShapeDtypeStruct((R, LANES), F32)]
    if has_f:
        out_specs = [pl.BlockSpec((tm, D), row)] + out_specs
        out_shape = [jax.ShapeDtypeStruct((R, D), F32)] + out_shape
    return pl.pallas_call(
        functools.partial(_inproj_kernel, has_f=has_f),
        grid=(R // tm, N // tn),
        in_specs=in_specs, out_specs=out_specs, out_shape=out_shape,
        scratch_shapes=[pltpu.VMEM((tm, D), BF16)],
        compiler_params=_cparams(("arbitrary", "arbitrary")),
        name="in_projection",
    )(*args)


def _mconv_kernel(x_ref, w_ref, b_ref, o_ref, *, k_scale):
    x = x_ref[0].astype(F32)
    T = x.shape[0]
    row = lax.broadcasted_iota(jnp.int32, x.shape, 0)
    x_prev = jnp.where(row == 0, 0.0, pltpu.roll(x, 1, 0))
    x_next = jnp.where(row == T - 1, 0.0, pltpu.roll(x, T - 1, 0))
    y = x_prev * w_ref[0:1, :] + x * w_ref[1:2, :] + x_next * w_ref[2:3, :] + b_ref[...]
    y = y * _sigmoid(y)
    scale = jnp.where(pl.program_id(1) >= pl.num_programs(1) // 2, k_scale, 1.0)
    o_ref[0] = (y * scale).astype(BF16)


def mlstm_conv(p3, conv_w, conv_b):
    B, T, _ = p3.shape
    tc = 256
    nct = 2 * M_WIDTH // tc
    return pl.pallas_call(
        functools.partial(_mconv_kernel, k_scale=M_HEAD_DIM ** -0.5),
        grid=(B, nct),
        in_specs=[pl.BlockSpec((1, T, tc), lambda b, j: (b, 0, j)),
                  pl.BlockSpec((3, tc), lambda b, j: (0, j)),
                  pl.BlockSpec((1, tc), lambda b, j: (0, j))],
        out_specs=pl.BlockSpec((1, T, tc), lambda b, j: (b, 0, j)),
        out_shape=jax.ShapeDtypeStruct((B, T, 2 * M_WIDTH), BF16),
        compiler_params=_cparams(("arbitrary", "arbitrary")),
        name="mlstm_conv",
    )(p3, conv_w, conv_b.reshape(1, -1))


MLSTM_BT = 2
MV_ROWS = M_HEAD_DIM + 16


def _mlstm_vt_kernel(v_ref, vt_ref):
    ones = jnp.ones((MV_ROWS - M_HEAD_DIM, M_CHUNK), BF16)
    for h in range(M_HEADS):
        vt_ref[0, 0, h, 0:M_HEAD_DIM, :] = v_ref[0, :, h * M_HEAD_DIM:(h + 1) * M_HEAD_DIM].astype(F32).T.astype(BF16)
        vt_ref[0, 0, h, M_HEAD_DIM:MV_ROWS, :] = ones


def mlstm_vt(p3):
    B, T, _ = p3.shape
    nc = T // M_CHUNK
    return pl.pallas_call(
        _mlstm_vt_kernel,
        grid=(B, nc),
        in_specs=[pl.BlockSpec((1, M_CHUNK, M_WIDTH), lambda b, c: (b, c, COL_MV // M_WIDTH))],
        out_specs=pl.BlockSpec((1, 1, M_HEADS, MV_ROWS, M_CHUNK), lambda b, c: (b, c, 0, 0, 0)),
        out_shape=jax.ShapeDtypeStruct((B, nc, M_HEADS, MV_ROWS, M_CHUNK), BF16),
        compiler_params=_cparams(("arbitrary", "arbitrary")),
        name="mlstm_vt",
    )(p3)


def _mlstm_kernel(q0_ref, k0_ref, v0_ref, g0_ref, q1_ref, k1_ref, v1_ref, g1_ref, cin_ref, min_ref,
                  h0_ref, h1_ref, cout_ref, mout_ref, c_scr, m_scr):
    c = pl.program_id(1)
    L = M_CHUNK

    @pl.when(c == 0)
    def _():
        c_scr[...] = cin_ref[...]
        m_scr[...] = min_ref[...]

    row = lax.broadcasted_iota(jnp.int32, (L, L), 0)
    col = lax.broadcasted_iota(jnp.int32, (L, L), 1)
    all_ones = jnp.ones((L, L), F32)
    dirs = ((q0_ref, k0_ref, v0_ref, g0_ref, h0_ref), (q1_ref, k1_ref, v1_ref, g1_ref, h1_ref))
    for bi, (d, (q_ref, k_ref, v_ref, g_ref, h_ref)) in [(bi, dr) for bi in range(q0_ref.shape[0])
                                                         for dr in enumerate(dirs)]:
        G = g_ref[bi]
        tri = jnp.where(col <= row, 1.0, 0.0) if d == 0 else jnp.where(col >= row, 1.0, 0.0)
        CS = jnp.dot(tri, G, precision=HIGHEST, preferred_element_type=F32)
        TOT = jnp.dot(all_ones, G, precision=HIGHEST, preferred_element_type=F32)
        GT, CST, TOTT = G.T, CS.T, TOT.T
        keep = (row <= col) if d == 0 else (row >= col)
        for h in range(M_HEADS):
            ci, cf, sidx = d * 8 + h, d * 8 + 4 + h, d * M_HEADS + h
            b_row, i_row, bend = CST[cf:cf + 1, :], GT[ci:ci + 1, :], TOTT[cf:cf + 1, :]
            col_s = G[:, ci:ci + 1] - CS[:, cf:cf + 1]
            m_prev = m_scr[bi, sidx:sidx + 1, :]
            hs = slice(h * M_HEAD_DIM, (h + 1) * M_HEAD_DIM)
            q = q_ref[bi, :, hs]
            k = k_ref[bi, :, hs]
            vt = v_ref[bi, 0, h]
            ct = c_scr[bi, sidx]

            dmat = jnp.where(keep, b_row + col_s, NEG)
            m_inter = b_row + m_prev
            m_t = jnp.maximum(jnp.max(dmat, axis=0, keepdims=True), m_inter)
            a_t = _nt_dot(k, q) * jnp.exp(dmat - m_t)
            inter = jnp.exp(m_inter - m_t)
            nd = (jnp.dot(vt, a_t.astype(BF16), preferred_element_type=F32)
                  + inter * _nt_dot(ct.astype(BF16), q))
            den = nd[M_HEAD_DIM:M_HEAD_DIM + 1, :]
            h_t = nd[0:M_HEAD_DIM, :] / jnp.maximum(jnp.abs(den), jnp.exp(-m_t))
            h_ref[bi, :, hs] = h_t.T.astype(BF16)

            g_row = bend - b_row + i_row
            m_new = jnp.maximum(bend + m_prev, jnp.max(g_row, axis=1, keepdims=True))
            decay = jnp.exp(bend + m_prev - m_new)
            vw = (vt.astype(F32) * jnp.exp(g_row - m_new)).astype(BF16)
            c_scr[bi, sidx] = decay * ct + jnp.dot(vw, k, preferred_element_type=F32)
            m_scr[bi, sidx:sidx + 1, :] = m_new

    @pl.when(c == pl.num_programs(1) - 1)
    def _():
        cout_ref[...] = c_scr[...]
        mout_ref[...] = m_scr[...]


def mlstm_scan(qk, vt, gates, c_in, m_in):
    B, T, _ = qk.shape
    nc = T // M_CHUNK
    W = M_WIDTH
    bt = MLSTM_BT
    fwd = lambda j: (lambda b, c: (b, c, j))
    bwd = lambda j: (lambda b, c: (b, nc - 1 - c, j))
    st4 = lambda b, c: (b, 0, 0, 0)
    st3 = lambda b, c: (b, 0, 0)
    blk = lambda w, im: pl.BlockSpec((bt, M_CHUNK, w), im)
    vblk = lambda rev: pl.BlockSpec((bt, 1, M_HEADS, MV_ROWS, M_CHUNK),
                                    lambda b, c: (b, nc - 1 - c if rev else c, 0, 0, 0))
    state = pl.BlockSpec((bt, 2 * M_HEADS, MV_ROWS, M_HEAD_DIM), st4)
    in_specs = [blk(W, fwd(0)), blk(W, fwd(1)), vblk(False), blk(LANES, fwd(0)),
                blk(W, bwd(0)), blk(W, bwd(1)), vblk(True), blk(LANES, bwd(0)),
                state, pl.BlockSpec((bt, 2 * M_HEADS, LANES), st3)]
    out_specs = [blk(W, fwd(0)), blk(W, bwd(0)), state, pl.BlockSpec((bt, 2 * M_HEADS, LANES), st3)]
    out_shape = [jax.ShapeDtypeStruct((B, T, W), BF16), jax.ShapeDtypeStruct((B, T, W), BF16),
                 jax.ShapeDtypeStruct(c_in.shape, F32), jax.ShapeDtypeStruct(m_in.shape, F32)]
    return pl.pallas_call(
        _mlstm_kernel,
        grid=(B // bt, nc),
        in_specs=in_specs, out_specs=out_specs, out_shape=out_shape,
        scratch_shapes=[pltpu.VMEM((bt, 2 * M_HEADS, MV_ROWS, M_HEAD_DIM), F32),
                        pltpu.VMEM((bt, 2 * M_HEADS, LANES), F32)],
        compiler_params=_cparams(("arbitrary", "arbitrary")),
        name="mlstm_scan",
    )(qk, qk, vt, gates, qk, qk, vt, gates, c_in, m_in)


VT_ROWS = LANES + 16


def _qkprep_kernel(*refs, do_norm, do_rope, q_scale, with_vt):
    if with_vt:
        q_ref, k_ref, v_ref, gq_ref, gk_ref, cos_ref, sin_ref, qo_ref, ko_ref, vt_ref = refs
        vt_ref[0, 0, 0:LANES, :] = v_ref[0].astype(F32).T.astype(BF16)
        vt_ref[0, 0, LANES:VT_ROWS, :] = jnp.ones((VT_ROWS - LANES, v_ref.shape[1]), BF16)
    else:
        q_ref, k_ref, gq_ref, gk_ref, cos_ref, sin_ref, qo_ref, ko_ref = refs
    tm = q_ref.shape[1]
    lane = lax.broadcasted_iota(jnp.int32, (tm, LANES), 1)
    lo = lane < HEAD_DIM
    first_half = (lane & (HEAD_DIM // 2)) == 0

    def proc(x, g):
        if do_norm:
            ss = x * x
            s_lo = jnp.sum(jnp.where(lo, ss, 0.0), axis=1, keepdims=True)
            s_hi = jnp.sum(jnp.where(lo, 0.0, ss), axis=1, keepdims=True)
            x = (x * lax.rsqrt(jnp.where(lo, s_lo, s_hi) / HEAD_DIM + EPS)) * g
        if do_rope:
            partner = jnp.where(first_half, pltpu.roll(x, LANES - HEAD_DIM // 2, 1), pltpu.roll(x, HEAD_DIM // 2, 1))
            x = x * cos_ref[...] + partner * sin_ref[...]
        return x

    for p in range(q_ref.shape[2] // LANES):
        sl = slice(p * LANES, (p + 1) * LANES)
        qo_ref[0, :, sl] = (proc(q_ref[0, :, sl].astype(F32), gq_ref[...]) * q_scale).astype(BF16)
    ko_ref[0] = proc(k_ref[0].astype(F32), gk_ref[...]).astype(BF16)


def qk_prep(p3, col_q, col_k, gq, gk, cos, sin, do_norm, do_rope, q_scale, col_v=None):
    B, T, _ = p3.shape
    tm = min(GATTN_TK, T)
    QW = N_QHEADS * HEAD_DIM
    fixed = lambda b, i: (0, 0)
    tile = lambda col: pl.BlockSpec((1, tm, LANES), lambda b, i: (b, i, col // LANES))
    with_vt = col_v is not None
    in_specs = [pl.BlockSpec((1, tm, QW), lambda b, i: (b, i, col_q // QW)), tile(col_k)]
    args = [p3, p3]
    out_specs = [pl.BlockSpec((1, tm, QW), lambda b, i: (b, i, 0)), pl.BlockSpec((1, tm, LANES), lambda b, i: (b, i, 0))]
    out_shape = [jax.ShapeDtypeStruct((B, T, QW), BF16), jax.ShapeDtypeStruct((B, T, LANES), BF16)]
    if with_vt:
        in_specs.append(tile(col_v))
        args.append(p3)
        out_specs.append(pl.BlockSpec((1, 1, VT_ROWS, tm), lambda b, i: (b, i, 0, 0)))
        out_shape.append(jax.ShapeDtypeStruct((B, T // tm, VT_ROWS, tm), BF16))
    in_specs += [pl.BlockSpec((1, LANES), fixed), pl.BlockSpec((1, LANES), fixed),
                 pl.BlockSpec((tm, LANES), lambda b, i: (i, 0)), pl.BlockSpec((tm, LANES), lambda b, i: (i, 0))]
    args += [gq, gk, cos, sin]
    return pl.pallas_call(
        functools.partial(_qkprep_kernel, do_norm=do_norm, do_rope=do_rope, q_scale=q_scale, with_vt=with_vt),
        grid=(B, T // tm),
        in_specs=in_specs, out_specs=out_specs, out_shape=out_shape,
        compiler_params=_cparams(("arbitrary", "arbitrary")),
        name="qk_prep",
    )(*args)


def _stack_pair(q):
    lo = _lo_mask(q.shape)
    zero = jnp.zeros_like(q)
    return jnp.concatenate([jnp.where(lo, q, zero), jnp.where(lo, zero, q)], axis=0)


def _unstack_pair(o, tq):
    return jnp.where(_lo_mask((tq, LANES)), o[:tq], o[tq:])


def _nt_dot(a, b):
    return lax.dot_general(a, b, (((1,), (1,)), ((), ())), preferred_element_type=F32)


GATTN_TQ = 256
GATTN_TK = 512
LOG2E = 1.4426950408889634


def _gattn_kernel(q_ref, k_ref, vt_ref, kc_ref, vtc_ref, o_ref, s_scr):
    tq = q_ref.shape[1]
    n_pairs = q_ref.shape[2] // LANES
    C = kc_ref.shape[1]
    qqs = [_stack_pair(q_ref[0, :, p * LANES:(p + 1) * LANES]) for p in range(n_pairs)]

    def scores(kb, slot, n):
        for p, qq in enumerate(qqs):
            s_scr[slot, p, 0:n, :] = _nt_dot(kb, qq)

    def update(carries, slot, n, vtb):
        out = []
        for p, (m, acc) in enumerate(carries):
            m_new = jnp.maximum(m, jnp.max(s_scr[slot, p, 0:n, :], axis=0, keepdims=True))
            alpha = jnp.exp2(m - m_new)
            e = jnp.exp2((s_scr[slot, p, 0:n, :] - m_new).astype(BF16))
            out.append((m_new, alpha * acc + jnp.dot(vtb, e, preferred_element_type=F32)))
        return tuple(out)

    def k_chunk(i):
        return k_ref[0, pl.ds(pl.multiple_of(i * GATTN_TK, GATTN_TK), GATTN_TK), :]

    n_chunks = vt_ref.shape[1]
    carries = tuple((jnp.full((1, 2 * tq), NEG, F32), jnp.zeros((VT_ROWS, 2 * tq), F32)) for _ in range(n_pairs))
    scores(kc_ref[0], 1, C)
    scores(k_chunk(0), 0, GATTN_TK)
    carries = update(carries, 1, C, vtc_ref[0, 0])

    def body(j, carries):
        scores(k_chunk(2 * j + 1), 1, GATTN_TK)
        carries = update(carries, 0, GATTN_TK, vt_ref[0, 2 * j])
        scores(k_chunk(2 * j + 2), 0, GATTN_TK)
        return update(carries, 1, GATTN_TK, vt_ref[0, 2 * j + 1])

    assert n_chunks % 2 == 0
    carries = lax.fori_loop(0, n_chunks // 2 - 1, body, carries)
    scores(k_chunk(n_chunks - 1), 1, GATTN_TK)
    carries = update(carries, 0, GATTN_TK, vt_ref[0, n_chunks - 2])
    carries = update(carries, 1, GATTN_TK, vt_ref[0, n_chunks - 1])
    for p, (m, acc) in enumerate(carries):
        o_t = acc[0:LANES, :] / acc[LANES:LANES + 1, :]
        o_ref[0, :, p * LANES:(p + 1) * LANES] = _unstack_pair(o_t.T, tq).astype(BF16)


def global_attention(q, k, vt, kc, vtc):
    B, S, QW = q.shape
    C = kc.shape[1]
    tq = GATTN_TQ
    nck = vt.shape[1]
    return pl.pallas_call(
        _gattn_kernel,
        grid=(B, S // tq),
        in_specs=[pl.BlockSpec((1, tq, QW), lambda b, i: (b, i, 0)),
                  pl.BlockSpec((1, S, LANES), lambda b, i: (b, 0, 0)),
                  pl.BlockSpec((1, nck, VT_ROWS, GATTN_TK), lambda b, i: (b, 0, 0, 0)),
                  pl.BlockSpec((1, C, LANES), lambda b, i: (b, 0, 0)),
                  pl.BlockSpec((1, 1, VT_ROWS, C), lambda b, i: (b, 0, 0, 0))],
        out_specs=pl.BlockSpec((1, tq, QW), lambda b, i: (b, i, 0)),
        out_shape=jax.ShapeDtypeStruct((B, S, QW), BF16),
        scratch_shapes=[pltpu.VMEM((2, QW // LANES, GATTN_TK, 2 * tq), F32)],
        compiler_params=_cparams(("arbitrary", "arbitrary")),
        name="global_attention",
    )(q, k, vt, kc, vtc)


def _wattn_kernel(sink_ref, q_ref, k_ref, v_ref, kc_ref, vc_ref, o_ref):
    tq = q_ref.shape[1]
    S = k_ref.shape[1]
    span = tq + 2 * WINDOW
    i = pl.program_id(1)
    start = pl.multiple_of(jnp.clip(i * tq - WINDOW, 0, S - span), WINDOW)
    kb = k_ref[0, pl.ds(start, span), :]
    vb = v_ref[0, pl.ds(start, span), :]
    kc = kc_ref[0]
    vc = vc_ref[0]
    qpos = i * tq + lax.broadcasted_iota(jnp.int32, (tq, span), 0)
    kpos = start + lax.broadcasted_iota(jnp.int32, (tq, span), 1)
    valid = jnp.abs(kpos - qpos) <= WINDOW
    valid2 = jnp.concatenate([valid, valid], axis=0)
    top = lax.broadcasted_iota(jnp.int32, (2 * tq, 1), 0) < tq
    for p in range(q_ref.shape[2] // LANES):
        sl = slice(p * LANES, (p + 1) * LANES)
        qq = _stack_pair(q_ref[0, :, sl])
        sink = jnp.where(top, sink_ref[PAIR_ORDER[2 * p]], sink_ref[PAIR_ORDER[2 * p + 1]])
        s_loc = jnp.where(valid2, _nt_dot(qq, kb), NEG)
        s_ctx = _nt_dot(qq, kc)
        m = jnp.maximum(jnp.maximum(jnp.max(s_loc, axis=1, keepdims=True), jnp.max(s_ctx, axis=1, keepdims=True)), sink)
        p_loc = jnp.exp(s_loc - m)
        p_ctx = jnp.exp(s_ctx - m)
        l = jnp.sum(p_loc, axis=1, keepdims=True) + jnp.sum(p_ctx, axis=1, keepdims=True) + jnp.exp(sink - m)
        o = (jnp.dot(p_loc.astype(BF16), vb, preferred_element_type=F32)
             + jnp.dot(p_ctx.astype(BF16), vc, preferred_element_type=F32)) / l
        o_ref[0, :, sl] = _unstack_pair(o, tq).astype(BF16)


def window_attention(sink, q, k, p3, p3c):
    B, S, QW = q.shape
    C = p3c.shape[1]
    tq = 256
    im = lambda j: (lambda b, i: (b, 0, j))
    return pl.pallas_call(
        _wattn_kernel,
        grid=(B, S // tq),
        in_specs=[pl.BlockSpec(memory_space=pltpu.SMEM),
                  pl.BlockSpec((1, tq, QW), lambda b, i: (b, i, 0)),
                  pl.BlockSpec((1, S, LANES), im(0)),
                  pl.BlockSpec((1, S, LANES), im(COL_WV // LANES)),
                  pl.BlockSpec((1, C, LANES), im(COL_WK // LANES)),
                  pl.BlockSpec((1, C, LANES), im(COL_WV // LANES))],
        out_specs=pl.BlockSpec((1, tq, QW), lambda b, i: (b, i, 0)),
        out_shape=jax.ShapeDtypeStruct((B, S, QW), BF16),
        compiler_params=_cparams(("arbitrary", "arbitrary")),
        name="window_attention",
    )(sink, q, k, p3, p3c, p3c)


NB_QROWS = 4
NB_KROWS = NB_QROWS + NB_ROWS


def _nattn_kernel(q_ref, k_ref, v_ref, kc_ref, vc_ref, tab_ref, o_ref):
    tq = q_ref.shape[1]
    rows = k_ref.shape[1] // GRID_W
    nk = NB_KROWS * GRID_W
    j = pl.program_id(1)
    start = pl.multiple_of(jnp.clip(NB_QROWS * j - NB_ROWS // 2, 0, rows - NB_KROWS) * GRID_W, GRID_W)
    for p in range(q_ref.shape[2] // LANES):
        sl = slice(p * LANES, (p + 1) * LANES)
        kb = k_ref[0, pl.ds(start, nk), sl]
        vb = v_ref[0, pl.ds(start, nk), sl]
        kc = kc_ref[0, :, sl]
        vc = vc_ref[0, :, sl]
        qq = _stack_pair(q_ref[0, :, sl] * (HEAD_DIM ** -0.5))
        s_loc = _nt_dot(qq, kb) + tab_ref[0, 2 * p:2 * p + 2].reshape(2 * tq, nk)
        s_ctx = _nt_dot(qq, kc)
        m = jnp.maximum(jnp.max(s_loc, axis=1, keepdims=True), jnp.max(s_ctx, axis=1, keepdims=True))
        p_loc = jnp.exp(s_loc - m)
        p_ctx = jnp.exp(s_ctx - m)
        l = jnp.sum(p_loc, axis=1, keepdims=True) + jnp.sum(p_ctx, axis=1, keepdims=True)
        o = (jnp.dot(p_loc.astype(BF16), vb, preferred_element_type=F32)
             + jnp.dot(p_ctx.astype(BF16), vc, preferred_element_type=F32)) / l
        o_ref[0, :, sl] = _unstack_pair(o, tq).astype(BF16)


def neighbourhood_bias_table(rel_bias, rows):
    nblk = rows // NB_QROWS
    H = rel_bias.shape[0]
    W = GRID_W
    pad = W - NB_COLS
    vp = jnp.pad(rel_bias.astype(F32), ((0, 0), (0, 0), (pad, pad)))
    toep = jnp.stack([vp[:, :, W - 1 - qc:2 * W - 1 - qc] for qc in range(W)], axis=2)
    qc = np.arange(W)[:, None]
    kc = np.arange(W)[None, :]
    c0 = np.clip(qc - NB_COLS // 2, 0, W - NB_COLS)
    toep = jnp.where(((kc >= c0) & (kc < c0 + NB_COLS))[None, None], toep, NEG)
    neg_block = jnp.full((H, W, W), NEG, F32)
    tabs = []
    for jrep in (0, 1, nblk - 1):
        kstart = int(np.clip(NB_QROWS * jrep - NB_ROWS // 2, 0, rows - NB_KROWS))
        q_rows = []
        for qr in range(NB_QROWS):
            r = NB_QROWS * jrep + qr
            r0 = int(np.clip(r - NB_ROWS // 2, 0, rows - NB_ROWS))
            blocks = []
            for kr in range(kstart, kstart + NB_KROWS):
                blocks.append(toep[:, kr - r + NB_ROWS - 1] if r0 <= kr < r0 + NB_ROWS else neg_block)
            q_rows.append(jnp.concatenate(blocks, axis=2))
        tabs.append(jnp.concatenate(q_rows, axis=1))
    return jnp.stack(tabs)


def neighbourhood_attention(p3, p3c, table):
    B, S, _ = p3.shape
    C = p3c.shape[1]
    QW = N_QHEADS * HEAD_DIM
    tq = NB_QROWS * GRID_W
    nblk = S // tq
    nk = NB_KROWS * GRID_W
    cls = lambda j: jnp.where(j == 0, 0, jnp.where(j == nblk - 1, 2, 1))
    im = lambda col: (lambda b, j: (b, 0, col // QW))
    return pl.pallas_call(
        _nattn_kernel,
        grid=(B, nblk),
        in_specs=[pl.BlockSpec((1, tq, QW), lambda b, j: (b, j, COL_NQ // QW)),
                  pl.BlockSpec((1, S, QW), im(COL_NK)), pl.BlockSpec((1, S, QW), im(COL_NV)),
                  pl.BlockSpec((1, C, QW), im(COL_NK)), pl.BlockSpec((1, C, QW), im(COL_NV)),
                  pl.BlockSpec((1, N_QHEADS, tq, nk), lambda b, j: (cls(j), 0, 0, 0))],
        out_specs=pl.BlockSpec((1, tq, QW), lambda b, j: (b, j, 0)),
        out_shape=jax.ShapeDtypeStruct((B, S, QW), BF16),
        compiler_params=_cparams(("arbitrary", "arbitrary")),
        name="neighbourhood_attention",
    )(p3, p3, p3, p3c, p3c, table)


def _cattn_kernel(sink_ref, q_ref, k_ref, v_ref, o_ref, *, kv_tiles, use_sink, q_scale):
    C = q_ref.shape[1]
    top = lax.broadcasted_iota(jnp.int32, (2 * C, 1), 0) < C
    for p in range(q_ref.shape[2] // LANES):
        sl = slice(p * LANES, (p + 1) * LANES)
        ksl = sl if kv_tiles > 1 else slice(0, LANES)
        q = q_ref[0, :, sl]
        if q_scale != 1.0:
            q = q * q_scale
        s = _nt_dot(_stack_pair(q), k_ref[0, :, ksl])
        m = jnp.max(s, axis=1, keepdims=True)
        if use_sink:
            sink = jnp.where(top, sink_ref[PAIR_ORDER[2 * p]], sink_ref[PAIR_ORDER[2 * p + 1]])
            m = jnp.maximum(m, sink)
        e = jnp.exp(s - m)
        l = jnp.sum(e, axis=1, keepdims=True)
        if use_sink:
            l = l + jnp.exp(sink - m)
        o = jnp.dot(e.astype(BF16), v_ref[0, :, ksl], preferred_element_type=F32) / l
        o_ref[0, :, sl] = _unstack_pair(o, C).astype(BF16)


def context_attention(sink, q_arr, q_col, k_arr, k_col, v_arr, v_col, kv_tiles, use_sink, q_scale):
    B, C, _ = q_arr.shape
    QW = N_QHEADS * HEAD_DIM
    KW = kv_tiles * LANES
    return pl.pallas_call(
        functools.partial(_cattn_kernel, kv_tiles=kv_tiles, use_sink=use_sink, q_scale=q_scale),
        grid=(B,),
        in_specs=[pl.BlockSpec(memory_space=pltpu.SMEM),
                  pl.BlockSpec((1, C, QW), lambda b: (b, 0, q_col // QW)),
                  pl.BlockSpec((1, C, KW), lambda b: (b, 0, k_col // KW)),
                  pl.BlockSpec((1, C, KW), lambda b: (b, 0, v_col // KW))],
        out_specs=pl.BlockSpec((1, C, QW), lambda b: (b, 0, 0)),
        out_shape=jax.ShapeDtypeStruct((B, C, QW), BF16),
        compiler_params=_cparams(("arbitrary",)),
        name="context_attention",
    )(sink, q_arr, k_arr, v_arr)


def _merge_kernel(h0_ref, h1_ref, mo_ref, ng_ref, ab_ref, ac_ref, ad_ref, g0_ref, g1_ref, g2_ref, g3_ref,
                  wb_ref, wo_ref, x_ref, ga_ref, xo_ref):
    hs = h0_ref[...].astype(F32) + h1_ref[...].astype(F32)
    parts = []
    for h in range(M_HEADS):
        hh = hs[:, h * M_HEAD_DIM:(h + 1) * M_HEAD_DIM]
        parts.append(hh * lax.rsqrt(jnp.mean(hh * hh, axis=1, keepdims=True) + EPS))
    a0 = (jnp.concatenate(parts, axis=1) * ng_ref[...]) * _sigmoid(mo_ref[...].astype(F32))
    branches = (a0.astype(BF16), ab_ref[...], ac_ref[...], ad_ref[...])
    gates = (g0_ref, g1_ref, g2_ref, g3_ref)
    y = None
    for i in range(N_BRANCH):
        t = _sigmoid(gates[i][...].astype(F32)) * jnp.dot(branches[i], wb_ref[i], preferred_element_type=F32)
        y = t if y is None else y + t
    out = jnp.dot(y.astype(BF16), wo_ref[...], preferred_element_type=F32)
    xo_ref[...] = x_ref[...] + ga_ref[0] * out


def merge_branches(h0, h1, p2, norm_g, att_b, att_c, att_d, w_branch, w_out, x, ga, rows_per_mod):
    R, D = x.shape
    tm = min(512, R)
    tpm = rows_per_mod // tm
    row = lambda i: (i, 0)
    fixed = lambda i: (0, 0)
    gate_spec = lambda k: pl.BlockSpec((tm, D), lambda i: (i, COL_GATE // D + k))
    return pl.pallas_call(
        _merge_kernel,
        grid=(R // tm,),
        in_specs=[pl.BlockSpec((tm, M_WIDTH), row), pl.BlockSpec((tm, M_WIDTH), row),
                  pl.BlockSpec((tm, M_WIDTH), lambda i: (i, COL_MO // M_WIDTH)),
                  pl.BlockSpec((1, M_WIDTH), fixed),
                  pl.BlockSpec((tm, BRANCH_WIDTH), row), pl.BlockSpec((tm, BRANCH_WIDTH), row),
                  pl.BlockSpec((tm, BRANCH_WIDTH), row),
                  gate_spec(0), gate_spec(1), gate_spec(2), gate_spec(3),
                  pl.BlockSpec((N_BRANCH, BRANCH_WIDTH, D), lambda i: (0, 0, 0)),
                  pl.BlockSpec((D, D), fixed),
                  pl.BlockSpec((tm, D), row),
                  pl.BlockSpec((1, 1, D), lambda i: (i // tpm, 0, 0))],
        out_specs=pl.BlockSpec((tm, D), row),
        out_shape=jax.ShapeDtypeStruct((R, D), F32),
        compiler_params=_cparams(("arbitrary",)),
        name="merge_branches",
    )(h0, h1, p2, norm_g.reshape(1, -1), att_b, att_c, att_d, p2, p2, p2, p2, w_branch, w_out, x, ga)


def _router_kernel(x_ref, g_ref, sc_ref, sh_ref, wr_ref, rb_ref, h_ref, e_ref, w_ref):
    x = x_ref[...]
    y = x * lax.rsqrt(jnp.mean(x * x, axis=-1, keepdims=True) + EPS)
    h = (y * g_ref[...]) * (1.0 + sc_ref[0]) + sh_ref[0]
    _to_token_tiles(h_ref, h, x.shape[0])
    logits = lax.dot_general(wr_ref[...], h, (((1,), (1,)), ((), ())), precision=HIGHEST,
                             preferred_element_type=F32)
    scores = _sigmoid(logits)
    sel = scores + rb_ref[...]
    E = EXPERTS_PER_GROUP
    tm = x.shape[0]
    sub = lax.broadcasted_iota(jnp.int32, (E, tm), 0)
    best = None
    for g in range(N_GROUPS):
        v = sel[g * E:(g + 1) * E]
        sc = scores[g * E:(g + 1) * E]
        m1 = jnp.max(v, axis=0, keepdims=True)
        i1 = jnp.min(jnp.where(v == m1, sub, E), axis=0, keepdims=True)
        rest = jnp.where(sub == i1, -jnp.inf, v)
        m2 = jnp.max(rest, axis=0, keepdims=True)
        i2 = jnp.min(jnp.where(rest == m2, sub, E), axis=0, keepdims=True)
        s1 = jnp.sum(jnp.where(sub == i1, sc, 0.0), axis=0, keepdims=True)
        s2 = jnp.sum(jnp.where(sub == i2, sc, 0.0), axis=0, keepdims=True)
        cand = (m1 + m2, g * E + i1, g * E + i2, s1, s2)
        if best is None:
            best = cand
        else:
            take = cand[0] > best[0]
            best = tuple(jnp.where(take, cn, bs) for cn, bs in zip(cand, best))
    _, e1, e2, s1, s2 = best
    tot = s1 + s2
    e_ref[0:1, :] = e1
    e_ref[1:2, :] = e2
    w_ref[0:1, :] = s1 / tot
    w_ref[1:2, :] = s2 / tot


def router(x, g, sc, sh, w_router_t, router_bias, rows_per_mod):
    R, D = x.shape
    tm = min(512, R)
    tpm = rows_per_mod // tm
    mod = lambda i: (i // tpm, 0, 0)
    fixed = lambda i: (0, 0)
    return pl.pallas_call(
        _router_kernel,
        grid=(R // tm,),
        in_specs=[pl.BlockSpec((tm, D), lambda i: (i, 0)), pl.BlockSpec((1, D), fixed),
                  pl.BlockSpec((1, 1, D), mod), pl.BlockSpec((1, 1, D), mod),
                  pl.BlockSpec((N_EXPERTS, D), fixed), pl.BlockSpec((N_EXPERTS, 1), fixed)],
        out_specs=[pl.BlockSpec((tm * ROW_TILE, LANES), lambda i: (i, 0)), pl.BlockSpec((2, tm), lambda i: (0, i)),
                   pl.BlockSpec((2, tm), lambda i: (0, i))],
        out_shape=[jax.ShapeDtypeStruct((R * ROW_TILE, LANES), F32), jax.ShapeDtypeStruct((2, R), jnp.int32),
                   jax.ShapeDtypeStruct((2, R), F32)],
        compiler_params=_cparams(("arbitrary",)),
        name="router",
    )(x, g.reshape(1, D), sc, sh, w_router_t, router_bias.reshape(-1, 1))


def _expert_kernel(be_ref, nu_ref, tok_ref, nxt_ref, dst_ref, h_hbm, wg_ref, wu_ref, wd_ref, y_hbm,
                   xbuf, ybuf, wgb, wub, wdb, gsem, ssem):
    i = pl.program_id(0)
    n_used = nu_ref[0]
    MB = MOE_BLOCK
    RT = ROW_TILE

    def gather_copy(t8, r, b):
        return pltpu.make_async_copy(h_hbm.at[pl.ds(pl.multiple_of(t8, RT), RT)], xbuf.at[b, pl.ds(r * RT, RT)],
                                     gsem.at[b])

    def scatter_copy(d8, r, b):
        return pltpu.make_async_copy(ybuf.at[b, pl.ds(r * RT, RT)], y_hbm.at[pl.ds(pl.multiple_of(d8, RT), RT)],
                                     ssem.at[b])

    def all_gathers(b):
        return pltpu.make_async_copy(h_hbm.at[pl.ds(0, MB * RT)], xbuf.at[b], gsem.at[b])

    def all_scatters(b):
        return pltpu.make_async_copy(ybuf.at[b], y_hbm.at[pl.ds(0, MB * RT)], ssem.at[b])

    def start_gather(idx_ref, b):
        for r in range(MB):
            gather_copy(idx_ref[0, 0, r], r, b).start(priority=r % 2)

    @pl.when(i == 0)
    def _():
        n_real = y_hbm.shape[0] - 2 * MB * RT
        ybuf[...] = jnp.zeros_like(ybuf)
        for b in range(2):
            fill = pltpu.make_async_copy(ybuf.at[b], y_hbm.at[pl.ds(n_real + b * MB * RT, MB * RT)], ssem.at[b])
            fill.start()
            fill.wait()

    @pl.when(jnp.logical_and(i == 0, n_used > 0))
    def _():
        start_gather(tok_ref, 0)

    for b in range(2):
        @pl.when(i % 2 == b)
        def _(b=b):
            @pl.when(jnp.logical_and(i == n_used, n_used > 0))
            def _():
                all_gathers(b).wait()

            @pl.when(jnp.logical_and(i >= 2, i - 2 < n_used))
            def _():
                all_scatters(b).wait()

            @pl.when(i < n_used)
            def _():
                prev = be_ref[jnp.maximum(i - 1, 0)]

                @pl.when(jnp.logical_or(i == 0, be_ref[i] != prev))
                def _():
                    wgb[...] = wg_ref[0, 0].astype(BF16)
                    wub[...] = wu_ref[0, 0].astype(BF16)
                    wdb[...] = wd_ref[0, 0].astype(BF16)

                all_gathers(b).wait()
                x = _from_token_tiles(xbuf.at[b], MB).astype(BF16)
                g = jnp.dot(x, wgb[...], preferred_element_type=F32)
                u = jnp.dot(x, wub[...], preferred_element_type=F32)
                hmid = ((g * _sigmoid(g)) * u).astype(BF16)
                start_gather(nxt_ref, 1 - b)
                _to_token_tiles(ybuf.at[b], jnp.dot(hmid, wdb[...], preferred_element_type=F32), MB)
                for r in range(MB):
                    scatter_copy(dst_ref[0, 0, r], r, b).start(priority=r % 2)


def expert_ffn(blk_e, n_used, slot_tok, slot_dst, h2, e_gate, e_up, e_down, layer, n_out_rows):
    nblk = slot_tok.shape[0]
    D = D_MODEL
    last = nblk - 1
    MB = MOE_BLOCK
    RT = ROW_TILE
    wmap = lambda i, be, nu: (layer, be[jnp.minimum(i, last)], 0, 0)
    smem_blk = lambda off: pl.BlockSpec((1, 1, MB), lambda i, be, nu: (jnp.minimum(i + off, last), 0, 0),
                                        memory_space=pltpu.SMEM)
    grid_spec = pltpu.PrefetchScalarGridSpec(
        num_scalar_prefetch=2,
        grid=(nblk + 2,),
        in_specs=[smem_blk(0), smem_blk(1), smem_blk(0),
                  pl.BlockSpec(memory_space=pl.ANY),
                  pl.BlockSpec((1, 1, D, D_EXPERT), wmap), pl.BlockSpec((1, 1, D, D_EXPERT), wmap),
                  pl.BlockSpec((1, 1, D_EXPERT, D), wmap)],
        out_specs=pl.BlockSpec(memory_space=pl.ANY),
        scratch_shapes=[pltpu.VMEM((2, MB * RT, LANES), F32), pltpu.VMEM((2, MB * RT, LANES), F32),
                        pltpu.VMEM((D, D_EXPERT), BF16), pltpu.VMEM((D, D_EXPERT), BF16),
                        pltpu.VMEM((D_EXPERT, D), BF16),
                        pltpu.SemaphoreType.DMA((2,)), pltpu.SemaphoreType.DMA((2,))],
    )
    return pl.pallas_call(
        _expert_kernel,
        grid_spec=grid_spec,
        out_shape=jax.ShapeDtypeStruct((n_out_rows * RT, LANES), F32),
        compiler_params=_cparams(("arbitrary",)),
        name="expert_ffn",
    )(blk_e, n_used, slot_tok, slot_tok, slot_dst, h2, e_gate, e_up, e_down)


def moe_ffn(h2, expert, e_gate, e_up, e_down, layer):
    T = expert.shape[1]
    MB = MOE_BLOCK
    n_assign = 2 * T
    e_flat = expert.reshape(-1)
    order = jnp.argsort(e_flat, stable=True).astype(jnp.int32)
    eids = jnp.arange(N_EXPERTS, dtype=jnp.int32)
    counts = jnp.sum((e_flat[:, None] == eids[None, :]).astype(jnp.int32), axis=0)
    starts = jnp.cumsum(counts) - counts
    padded = ((counts + MB - 1) // MB) * MB
    pends = jnp.cumsum(padded)
    pstarts = pends - padded
    nblk = -(-(n_assign + N_EXPERTS * (MB - 1)) // MB)
    cap = nblk * MB
    slot = jnp.arange(cap, dtype=jnp.int32)
    slot_e = jnp.sum((slot[:, None] >= pends[None, :]).astype(jnp.int32), axis=1)
    onehot = slot_e[:, None] == eids[None, :]
    pick = lambda v: jnp.sum(jnp.where(onehot, v[None, :], 0), axis=1)
    local = slot - pick(pstarts)
    real = jnp.logical_and(slot_e < N_EXPERTS, local < pick(counts))
    slot_assign = order[jnp.clip(pick(starts) + local, 0, n_assign - 1)]
    dump = n_assign + ((slot // MB) % 2) * MB + slot % MB
    slot_dst = (jnp.where(real, slot_assign, dump) * ROW_TILE).reshape(nblk, 1, MB)
    slot_tok = (jnp.where(real, slot_assign % T, 0) * ROW_TILE).reshape(nblk, 1, MB)
    blk_e = jnp.minimum(slot_e.reshape(nblk, MB)[:, 0], N_EXPERTS - 1)
    n_used = (pends[-1] // MB).astype(jnp.int32).reshape(1)
    return expert_ffn(blk_e, n_used, slot_tok, slot_dst, h2, e_gate, e_up, e_down, layer, n_assign + 2 * MB)


def _final_kernel(x_ref, y0_ref, y1_ref, wt_ref, ga_ref, g_ref, o_ref):
    x = x_ref[...] + ga_ref[0] * _moe_combine(y0_ref, y1_ref, wt_ref)
    o_ref[...] = (x * lax.rsqrt(jnp.mean(x * x, axis=-1, keepdims=True) + EPS)) * g_ref[...]


def final_norm(x, moe, ga, g, rows_per_mod):
    R, D = x.shape
    tm = min(1024, R)
    tpm = rows_per_mod // tm
    row = lambda i: (i, 0)
    moe_specs, moe_args = _moe_specs(moe, tm, 1)
    return pl.pallas_call(
        _final_kernel,
        grid=(R // tm,),
        in_specs=[pl.BlockSpec((tm, D), row)] + moe_specs
        + [pl.BlockSpec((1, 1, D), lambda i: (i // tpm, 0, 0)), pl.BlockSpec((1, D), lambda i: (0, 0))],
        out_specs=pl.BlockSpec((tm, D), row),
        out_shape=jax.ShapeDtypeStruct((R, D), F32),
        compiler_params=_cparams(("arbitrary",)),
        name="final_norm",
    )(x, *moe_args, ga, g.reshape(1, D))


_DEINTERLEAVE = np.concatenate([np.arange(0, HEAD_DIM, 2), np.arange(1, HEAD_DIM, 2)])


def _head_cols(n_heads, order, perm):
    return np.concatenate([h * HEAD_DIM + perm for h in order])


def _reorder_w_in(w_in):
    widths = (512, 512, 512, 512, 16, 512, 128, 128, 512, 128, 128, 512, 512, 512, 4096)
    offs = np.concatenate([[0], np.cumsum(widths)])
    (mq, mk, mv, mo, mif, gq, gk, gv, wq, wk, wv, nq, nk, nv, gate) = [
        w_in[:, offs[i]:offs[i + 1]] for i in range(len(widths))]
    D = w_in.shape[0]
    half = HEAD_DIM // 2

    def deinterleave(a, n_heads):
        return a.reshape(D, n_heads, half, 2).transpose(0, 1, 3, 2).reshape(D, n_heads * HEAD_DIM)

    def pair_heads(a):
        return a.reshape(D, 2, N_QHEADS // 2, HEAD_DIM).transpose(0, 2, 1, 3).reshape(D, N_QHEADS * HEAD_DIM)

    w = jnp.concatenate([mq, mk, mv, mo, pair_heads(deinterleave(gq, N_QHEADS)), pair_heads(deinterleave(wq, N_QHEADS)),
                         nq, nk, nv, deinterleave(gk, 2), gv, deinterleave(wk, 2), wv, gate], axis=1).astype(BF16)
    wif = jnp.pad(mif, ((0, 0), (0, LANES - mif.shape[1]))).astype(BF16)
    return w, wif


def _rope_tables(S):
    t = np.arange(S)
    n_freq = HEAD_DIM // 4
    inv_freq = jnp.asarray(ROPE_THETA, F32) ** (-jnp.arange(n_freq, dtype=F32) / n_freq)
    row = jnp.asarray(t // GRID_W, F32)
    col = jnp.asarray(t % GRID_W, F32)
    ang = jnp.concatenate([row[:, None] * inv_freq, col[:, None] * inv_freq], axis=-1)
    cos, sin = jnp.cos(ang), jnp.sin(ang)
    cos_t = jnp.tile(jnp.concatenate([cos, cos], axis=1), (1, 2))
    sin_t = jnp.tile(jnp.concatenate([-sin, sin], axis=1), (1, 2))
    return cos_t, sin_t


def kernel(x, c, ctx, c_ctx, w_mod, b_mod, norm1_g, norm2_g, w_in, m_conv_w, m_conv_b, m_gate_b, m_norm_g,
           g_qnorm, g_knorm, w_sink, n_rel_bias, w_branch, w_out, w_router, router_bias, e_gate, e_up, e_down,
           final_g):
    B, S, D = x.shape
    C = ctx.shape[1]
    L = w_mod.shape[0]
    R, RC = B * S, B * C
    rows = S // GRID_W

    c_all = jnp.zeros((16, D), F32).at[:B].set(c).at[B].set(c_ctx)
    mod = modulation(c_all, w_mod, b_mod).reshape(L, 16, 6, D)
    cos_t, sin_t = _rope_tables(S)
    w_router_t = w_router.T

    xl = x.reshape(R, D)
    xc = ctx.reshape(RC, D)
    moe_lat = moe_ctx = ga2_lat = ga2_ctx = None
    for l in range(L):
        need_ctx = l < L - 1
        lat = [mod[l, :B, i].reshape(B, 1, D) for i in range(6)]
        cx = [mod[l, B:B + 1, i].reshape(1, 1, D) for i in range(6)]
        w, wif = _reorder_w_in(w_in[l])
        gb = jnp.pad(m_gate_b[l].reshape(1, -1), ((0, 0), (0, LANES - 16)))
        gq = jnp.tile(g_qnorm[l][_DEINTERLEAVE], 2).reshape(1, LANES)
        gk = jnp.tile(g_knorm[l][_DEINTERLEAVE], 2).reshape(1, LANES)
        pair_rows = lambda a: a.reshape(2, N_QHEADS // 2, HEAD_DIM, D).transpose(1, 0, 2, 3).reshape(BRANCH_WIDTH, D)
        wb = jnp.stack([w_branch[l, 0], pair_rows(w_branch[l, 1]), pair_rows(w_branch[l, 2]),
                        w_branch[l, 3]]).astype(BF16)
        wo = w_out[l].astype(BF16)

        if l == 0:
            p_lat, g_lat = in_projection(xl, norm1_g[l], lat[1], lat[0], w, wif, gb, S)
            p_ctx, g_ctx = in_projection(xc, norm1_g[l], cx[1], cx[0], w, wif, gb, RC)
        else:
            xl, p_lat, g_lat = in_projection(xl, norm1_g[l], lat[1], lat[0], w, wif, gb, S, moe=moe_lat, ga=ga2_lat)
            xc, p_ctx, g_ctx = in_projection(xc, norm1_g[l], cx[1], cx[0], w, wif, gb, RC, moe=moe_ctx, ga=ga2_ctx)
        p3 = p_lat.reshape(B, S, -1)
        p3c = p_ctx.reshape(B, C, -1)

        qk_c = mlstm_conv(p3c, m_conv_w[l], m_conv_b[l])
        qk_l = mlstm_conv(p3, m_conv_w[l], m_conv_b[l])
        c0 = jnp.zeros((B, 2 * M_HEADS, MV_ROWS, M_HEAD_DIM), F32)
        m0 = jnp.full((B, 2 * M_HEADS, LANES), M_INIT, F32)
        hc0, hc1, c1, m1 = mlstm_scan(qk_c, mlstm_vt(p3c), g_ctx.reshape(B, C, LANES), c0, m0)
        hl0, hl1, _, _ = mlstm_scan(qk_l, mlstm_vt(p3), g_lat.reshape(B, S, LANES), c1, m1)

        scale = HEAD_DIM ** -0.5
        gq_l, gk_l, vt_l = qk_prep(p3, COL_GQ, COL_GK, gq, gk, cos_t, sin_t, True, True, scale * LOG2E, col_v=COL_GV)
        gq_c, gk_c, vt_c = qk_prep(p3c, COL_GQ, COL_GK, gq, gk, cos_t[:C], sin_t[:C], True, False, scale, col_v=COL_GV)
        att_b = global_attention(gq_l, gk_l, vt_l, gk_c, vt_c)

        wq_l, wk_l = qk_prep(p3, COL_WQ, COL_WK, gq, gk, cos_t, sin_t, False, True, scale)
        att_c = window_attention(w_sink[l], wq_l, wk_l, p3, p3c)

        table = neighbourhood_bias_table(n_rel_bias[l], rows)
        att_d = neighbourhood_attention(p3, p3c, table)

        xl = merge_branches(hl0.reshape(R, -1), hl1.reshape(R, -1), p_lat, m_norm_g[l], att_b.reshape(R, -1),
                            att_c.reshape(R, -1), att_d.reshape(R, -1), wb, wo, xl, lat[2], S)
        h2, ex, wt = router(xl, norm2_g[l], lat[4], lat[3], w_router_t, router_bias, S)
        if need_ctx:
            cb = context_attention(w_sink[l], gq_c, 0, gk_c, 0, p3c, COL_GV, 1, False, 1.0)
            cc = context_attention(w_sink[l], p3c, COL_WQ, p3c, COL_WK, p3c, COL_WV, 1, True, scale)
            cd = context_attention(w_sink[l], p3c, COL_NQ, p3c, COL_NK, p3c, COL_NV, 4, False, scale)
            xc = merge_branches(hc0.reshape(RC, -1), hc1.reshape(RC, -1), p_ctx, m_norm_g[l], cb.reshape(RC, -1),
                                cc.reshape(RC, -1), cd.reshape(RC, -1), wb, wo, xc, cx[2], RC)
            h2c, exc, wtc = router(xc, norm2_g[l], cx[4], cx[3], w_router_t, router_bias, RC)
            h2 = jnp.concatenate([h2, h2c], axis=0)
            ex = jnp.concatenate([ex, exc], axis=1)
            wt = jnp.concatenate([wt, wtc], axis=1)
        y = moe_ffn(h2, ex, e_gate, e_up, e_down, l)
        wt_t = wt.T
        moe_lat, ga2_lat = (y, wt_t, 0), lat[5]
        if need_ctx:
            moe_ctx, ga2_ctx = (y, wt_t, R), cx[5]
    out = final_norm(xl, moe_lat, ga2_lat, final_g, S)
    return out.reshape(B, S, D)
```

```python
import functools

import numpy as np
import jax
import jax.numpy as jnp
from jax import lax
from jax.experimental import pallas as pl
from jax.experimental.pallas import tpu as pltpu

F32 = jnp.float32
BF16 = jnp.bfloat16
HIGHEST = lax.Precision.HIGHEST

D_MODEL = 1024
DEPTH = 2
GRID_W = 64
HEAD_DIM = 64
ROPE_THETA = 10000.0
EPS = 1e-6
M_INIT = -1e30
NEG = -1e30
M_HEADS = 4
M_HEAD_DIM = 128
M_WIDTH = M_HEADS * M_HEAD_DIM
M_CHUNK = 128
N_QHEADS = 8
WINDOW = 128
NB_ROWS = 8
NB_COLS = 16
N_BRANCH = 4
BRANCH_WIDTH = 512
N_EXPERTS = 32
N_GROUPS = 4
EXPERTS_PER_GROUP = N_EXPERTS // N_GROUPS
D_EXPERT = 512
MOE_BLOCK = 256
LANES = 128
VMEM_LIMIT = 56 * 1024 * 1024

COL_MQ, COL_MK, COL_MV, COL_MO = 0, 512, 1024, 1536
COL_GQ, COL_WQ, COL_NQ, COL_NK, COL_NV = 2048, 2560, 3072, 3584, 4096
COL_GK, COL_GV, COL_WK, COL_WV = 4608, 4736, 4864, 4992
COL_GATE = 5120
N_PROJ_OUT = COL_GATE + N_BRANCH * D_MODEL
PAIR_ORDER = (0, 4, 1, 5, 2, 6, 3, 7)


def _cparams(sem):
    return pltpu.CompilerParams(dimension_semantics=sem, vmem_limit_bytes=VMEM_LIMIT)


def _sigmoid(x):
    return 1.0 / (1.0 + jnp.exp(-x))


def _lo_mask(shape):
    return lax.broadcasted_iota(jnp.int32, shape, len(shape) - 1) < HEAD_DIM


def _mod_kernel(c_ref, w_ref, b_ref, o_ref):
    c = c_ref[...]
    a = c * _sigmoid(c)
    o_ref[0] = jnp.dot(a, w_ref[0], precision=HIGHEST, preferred_element_type=F32) + b_ref[0]


def modulation(c_all, w_mod, b_mod):
    L, D, N = w_mod.shape
    R = c_all.shape[0]
    tn = 1536
    return pl.pallas_call(
        _mod_kernel,
        grid=(L, N // tn),
        in_specs=[pl.BlockSpec((R, D), lambda l, j: (0, 0)),
                  pl.BlockSpec((1, D, tn), lambda l, j: (l, 0, j)),
                  pl.BlockSpec((1, 1, tn), lambda l, j: (l, 0, j))],
        out_specs=pl.BlockSpec((1, R, tn), lambda l, j: (l, 0, j)),
        out_shape=jax.ShapeDtypeStruct((L, R, N), F32),
        compiler_params=_cparams(("arbitrary", "arbitrary")),
        name="modulation",
    )(c_all, w_mod, b_mod.reshape(L, 1, N))


def _log_sigmoid(z):
    return jnp.minimum(z, 0.0) - jnp.log(1.0 + jnp.exp(-jnp.abs(z)))


ROW_TILE = D_MODEL // LANES


def _to_token_tiles(ref, val, n):
    for j in range(ROW_TILE):
        ref[pl.ds(j, n, stride=ROW_TILE), :] = val[:, j * LANES:(j + 1) * LANES]


def _from_token_tiles(ref, n):
    return jnp.concatenate([ref[pl.ds(j, n, stride=ROW_TILE), :] for j in range(ROW_TILE)], axis=1)


def _moe_combine(y0_ref, y1_ref, wt_ref):
    wt = wt_ref[...]
    n = wt.shape[0]
    return wt[:, 0:1] * _from_token_tiles(y0_ref, n) + wt[:, 1:2] * _from_token_tiles(y1_ref, n)


def _inproj_kernel(*refs, has_f):
    if has_f:
        (x_ref, y0_ref, y1_ref, wt_ref, ga_ref, g_ref, sc_ref, sh_ref, w_ref, wif_ref, gb_ref,
         xo_ref, p_ref, gate_ref, h_scr) = refs
    else:
        x_ref, g_ref, sc_ref, sh_ref, w_ref, wif_ref, gb_ref, p_ref, gate_ref, h_scr = refs

    @pl.when(pl.program_id(1) == 0)
    def _():
        x = x_ref[...]
        if has_f:
            x = x + ga_ref[0] * _moe_combine(y0_ref, y1_ref, wt_ref)
            xo_ref[...] = x
        y = x * lax.rsqrt(jnp.mean(x * x, axis=-1, keepdims=True) + EPS)
        h = (y * g_ref[...]) * (1.0 + sc_ref[0]) + sh_ref[0]
        hb = h.astype(BF16)
        h_scr[...] = hb
        z = jnp.dot(hb, wif_ref[...], preferred_element_type=F32) + gb_ref[...]
        lane = lax.broadcasted_iota(jnp.int32, z.shape, 1)
        is_forget = (lane & 0xF4) == 4
        gate_ref[...] = jnp.where(is_forget, _log_sigmoid(z), z)

    p_ref[...] = jnp.dot(h_scr[...], w_ref[...], preferred_element_type=F32).astype(BF16)


def _moe_specs(moe, tm, nidx):
    y, wt, row0 = moe
    T = wt.shape[0]
    assert row0 % tm == 0 and T % tm == 0, (row0, T, tm)
    b0, b1 = row0 // tm, (T + row0) // tm
    if nidx == 2:
        maps = (lambda i, j: (b0 + i, 0), lambda i, j: (b1 + i, 0))
    else:
        maps = (lambda i: (b0 + i, 0), lambda i: (b1 + i, 0))
    tile_rows = pl.BlockSpec((tm * ROW_TILE, LANES), maps[0]), pl.BlockSpec((tm * ROW_TILE, LANES), maps[1])
    return [*tile_rows, pl.BlockSpec((tm, 2), maps[0])], [y, y, wt]


def in_projection(x, g, sc, sh, w, wif, gb, rows_per_mod, moe=None, ga=None):
    R, D = x.shape
    N = w.shape[1]
    tm = min(1024, R)
    tn = 1024
    tpm = rows_per_mod // tm
    has_f = moe is not None
    row = lambda i, j: (i, 0)
    mod = lambda i, j: (i // tpm, 0, 0)
    fixed = lambda i, j: (0, 0)
    in_specs = [pl.BlockSpec((tm, D), row)]
    args = [x]
    if has_f:
        moe_specs, moe_args = _moe_specs(moe, tm, 2)
        in_specs += moe_specs + [pl.BlockSpec((1, 1, D), mod)]
        args += moe_args + [ga]
    in_specs += [pl.BlockSpec((1, D), fixed), pl.BlockSpec((1, 1, D), mod), pl.BlockSpec((1, 1, D), mod),
                 pl.BlockSpec((D, tn), lambda i, j: (0, j)), pl.BlockSpec((D, LANES), fixed),
                 pl.BlockSpec((1, LANES), fixed)]
    args += [g.reshape(1, D), sc, sh, w, wif, gb]
    out_specs = [pl.BlockSpec((tm, tn), lambda i, j: (i, j)), pl.BlockSpec((tm, LANES), row)]
    out_shape = [jax.ShapeDtypeStruct((R, N), BF16), jax.ShapeDtypeStruct((R, LANES), F32)]
    if has_f:
        out_specs = [pl.BlockSpec((tm, D), row)] + out_specs
        out_shape = [jax.ShapeDtypeStruct((R, D), F32)] + out_shape
    return pl.pallas_call(
        functools.partial(_inproj_kernel, has_f=has_f),
        grid=(R // tm, N // tn),
        in_specs=in_specs, out_specs=out_specs, out_shape=out_shape,
        scratch_shapes=[pltpu.VMEM((tm, D), BF16)],
        compiler_params=_cparams(("arbitrary", "arbitrary")),
        name="in_projection",
    )(*args)


def _mconv_kernel(x_ref, w_ref, b_ref, o_ref, *, k_scale):
    x = x_ref[0].astype(F32)
    T = x.shape[0]
    row = lax.broadcasted_iota(jnp.int32, x.shape, 0)
    x_prev = jnp.where(row == 0, 0.0, pltpu.roll(x, 1, 0))
    x_next = jnp.where(row == T - 1, 0.0, pltpu.roll(x, T - 1, 0))
    y = x_prev * w_ref[0:1, :] + x * w_ref[1:2, :] + x_next * w_ref[2:3, :] + b_ref[...]
    y = y * _sigmoid(y)
    scale = jnp.where(pl.program_id(1) >= pl.num_programs(1) // 2, k_scale, 1.0)
    o_ref[0] = (y * scale).astype(BF16)


def mlstm_conv(p3, conv_w, conv_b):
    B, T, _ = p3.shape
    tc = 256
    nct = 2 * M_WIDTH // tc
    return pl.pallas_call(
        functools.partial(_mconv_kernel, k_scale=M_HEAD_DIM ** -0.5),
        grid=(B, nct),
        in_specs=[pl.BlockSpec((1, T, tc), lambda b, j: (b, 0, j)),
                  pl.BlockSpec((3, tc), lambda b, j: (0, j)),
                  pl.BlockSpec((1, tc), lambda b, j: (0, j))],
        out_specs=pl.BlockSpec((1, T, tc), lambda b, j: (b, 0, j)),
        out_shape=jax.ShapeDtypeStruct((B, T, 2 * M_WIDTH), BF16),
        compiler_params=_cparams(("arbitrary", "arbitrary")),
        name="mlstm_conv",
    )(p3, conv_w, conv_b.reshape(1, -1))


MLSTM_BT = 2
MV_ROWS = M_HEAD_DIM + 16


def _mlstm_vt_kernel(v_ref, vt_ref):
    ones = jnp.ones((MV_ROWS - M_HEAD_DIM, M_CHUNK), BF16)
    for c in range(vt_ref.shape[1]):
        for h in range(M_HEADS):
            v = v_ref[0, c * M_CHUNK:(c + 1) * M_CHUNK, h * M_HEAD_DIM:(h + 1) * M_HEAD_DIM]
            vt_ref[0, c, h, 0:M_HEAD_DIM, :] = v.astype(F32).T.astype(BF16)
            vt_ref[0, c, h, M_HEAD_DIM:MV_ROWS, :] = ones


def mlstm_vt(p3):
    B, T, _ = p3.shape
    nc = T // M_CHUNK
    cb = min(8, nc)
    return pl.pallas_call(
        _mlstm_vt_kernel,
        grid=(B, nc // cb),
        in_specs=[pl.BlockSpec((1, cb * M_CHUNK, M_WIDTH), lambda b, c: (b, c, COL_MV // M_WIDTH))],
        out_specs=pl.BlockSpec((1, cb, M_HEADS, MV_ROWS, M_CHUNK), lambda b, c: (b, c, 0, 0, 0)),
        out_shape=jax.ShapeDtypeStruct((B, nc, M_HEADS, MV_ROWS, M_CHUNK), BF16),
        compiler_params=_cparams(("arbitrary", "arbitrary")),
        name="mlstm_vt",
    )(p3)


def _mlstm_kernel(q0_ref, k0_ref, v0_ref, g0_ref, q1_ref, k1_ref, v1_ref, g1_ref, cin_ref, min_ref,
                  h0_ref, h1_ref, cout_ref, mout_ref, c_scr, m_scr):
    c = pl.program_id(1)
    L = M_CHUNK

    @pl.when(c == 0)
    def _():
        c_scr[...] = cin_ref[...]
        m_scr[...] = min_ref[...]

    row = lax.broadcasted_iota(jnp.int32, (L, L), 0)
    col = lax.broadcasted_iota(jnp.int32, (L, L), 1)
    all_ones = jnp.ones((L, L), F32)
    dirs = ((q0_ref, k0_ref, v0_ref, g0_ref, h0_ref), (q1_ref, k1_ref, v1_ref, g1_ref, h1_ref))
    for bi, (d, (q_ref, k_ref, v_ref, g_ref, h_ref)) in [(bi, dr) for bi in range(q0_ref.shape[0])
                                                         for dr in enumerate(dirs)]:
        G = g_ref[bi]
        tri = jnp.where(col <= row, 1.0, 0.0) if d == 0 else jnp.where(col >= row, 1.0, 0.0)
        CS = jnp.dot(tri, G, precision=HIGHEST, preferred_element_type=F32)
        TOT = jnp.dot(all_ones, G, precision=HIGHEST, preferred_element_type=F32)
        GT, CST, TOTT = G.T, CS.T, TOT.T
        keep = (row <= col) if d == 0 else (row >= col)
        for h in range(M_HEADS):
            ci, cf, sidx = d * 8 + h, d * 8 + 4 + h, d * M_HEADS + h
            b_row, i_row, bend = CST[cf:cf + 1, :], GT[ci:ci + 1, :], TOTT[cf:cf + 1, :]
            col_s = G[:, ci:ci + 1] - CS[:, cf:cf + 1]
            m_prev = m_scr[bi, sidx:sidx + 1, :]
            hs = slice(h * M_HEAD_DIM, (h + 1) * M_HEAD_DIM)
            q = q_ref[bi, :, hs]
            k = k_ref[bi, :, hs]
            vt = v_ref[bi, 0, h]
            ct = c_scr[bi, sidx]

            dmat = jnp.where(keep, b_row + col_s, NEG)
            m_inter = b_row + m_prev
            m_t = jnp.maximum(jnp.max(dmat, axis=0, keepdims=True), m_inter)
            a_t = _nt_dot(k, q) * jnp.exp(dmat - m_t)
            inter = jnp.exp(m_inter - m_t)
            nd = (jnp.dot(vt, a_t.astype(BF16), preferred_element_type=F32)
                  + inter * _nt_dot(ct.astype(BF16), q))
            den = nd[M_HEAD_DIM:M_HEAD_DIM + 1, :]
            h_t = nd[0:M_HEAD_DIM, :] / jnp.maximum(jnp.abs(den), jnp.exp(-m_t))
            h_ref[bi, :, hs] = h_t.T.astype(BF16)

            g_row = bend - b_row + i_row
            m_new = jnp.maximum(bend + m_prev, jnp.max(g_row, axis=1, keepdims=True))
            decay = jnp.exp(bend + m_prev - m_new)
            vw = (vt.astype(F32) * jnp.exp(g_row - m_new)).astype(BF16)
            c_scr[bi, sidx] = decay * ct + jnp.dot(vw, k, preferred_element_type=F32)
            m_scr[bi, sidx:sidx + 1, :] = m_new

    @pl.when(c == pl.num_programs(1) - 1)
    def _():
        cout_ref[...] = c_scr[...]
        mout_ref[...] = m_scr[...]


def mlstm_scan(qk, vt, gates, c_in, m_in):
    B, T, _ = qk.shape
    nc = T // M_CHUNK
    W = M_WIDTH
    bt = MLSTM_BT
    fwd = lambda j: (lambda b, c: (b, c, j))
    bwd = lambda j: (lambda b, c: (b, nc - 1 - c, j))
    st4 = lambda b, c: (b, 0, 0, 0)
    st3 = lambda b, c: (b, 0, 0)
    blk = lambda w, im: pl.BlockSpec((bt, M_CHUNK, w), im)
    vblk = lambda rev: pl.BlockSpec((bt, 1, M_HEADS, MV_ROWS, M_CHUNK),
                                    lambda b, c: (b, nc - 1 - c if rev else c, 0, 0, 0))
    state = pl.BlockSpec((bt, 2 * M_HEADS, MV_ROWS, M_HEAD_DIM), st4)
    in_specs = [blk(W, fwd(0)), blk(W, fwd(1)), vblk(False), blk(LANES, fwd(0)),
                blk(W, bwd(0)), blk(W, bwd(1)), vblk(True), blk(LANES, bwd(0)),
                state, pl.BlockSpec((bt, 2 * M_HEADS, LANES), st3)]
    out_specs = [blk(W, fwd(0)), blk(W, bwd(0)), state, pl.BlockSpec((bt, 2 * M_HEADS, LANES), st3)]
    out_shape = [jax.ShapeDtypeStruct((B, T, W), BF16), jax.ShapeDtypeStruct((B, T, W), BF16),
                 jax.ShapeDtypeStruct(c_in.shape, F32), jax.ShapeDtypeStruct(m_in.shape, F32)]
    return pl.pallas_call(
        _mlstm_kernel,
        grid=(B // bt, nc),
        in_specs=in_specs, out_specs=out_specs, out_shape=out_shape,
        scratch_shapes=[pltpu.VMEM((bt, 2 * M_HEADS, MV_ROWS, M_HEAD_DIM), F32),
                        pltpu.VMEM((bt, 2 * M_HEADS, LANES), F32)],
        compiler_params=_cparams(("arbitrary", "arbitrary")),
        name="mlstm_scan",
    )(qk, qk, vt, gates, qk, qk, vt, gates, c_in, m_in)


VT_ROWS = LANES + 16


def _qkprep_kernel(*refs, do_norm, do_rope, q_scale, with_vt):
    if with_vt:
        q_ref, k_ref, v_ref, gq_ref, gk_ref, cos_ref, sin_ref, qo_ref, ko_ref, vt_ref = refs
        vt_ref[0, 0, 0:LANES, :] = v_ref[0].astype(F32).T.astype(BF16)
        vt_ref[0, 0, LANES:VT_ROWS, :] = jnp.ones((VT_ROWS - LANES, v_ref.shape[1]), BF16)
    else:
        q_ref, k_ref, gq_ref, gk_ref, cos_ref, sin_ref, qo_ref, ko_ref = refs
    tm = q_ref.shape[1]
    lane = lax.broadcasted_iota(jnp.int32, (tm, LANES), 1)
    lo = lane < HEAD_DIM
    first_half = (lane & (HEAD_DIM // 2)) == 0

    def proc(x, g):
        if do_norm:
            ss = x * x
            s_lo = jnp.sum(jnp.where(lo, ss, 0.0), axis=1, keepdims=True)
            s_hi = jnp.sum(jnp.where(lo, 0.0, ss), axis=1, keepdims=True)
            x = (x * lax.rsqrt(jnp.where(lo, s_lo, s_hi) / HEAD_DIM + EPS)) * g
        if do_rope:
            partner = jnp.where(first_half, pltpu.roll(x, LANES - HEAD_DIM // 2, 1), pltpu.roll(x, HEAD_DIM // 2, 1))
            x = x * cos_ref[...] + partner * sin_ref[...]
        return x

    for p in range(q_ref.shape[2] // LANES):
        sl = slice(p * LANES, (p + 1) * LANES)
        qo_ref[0, :, sl] = (proc(q_ref[0, :, sl].astype(F32), gq_ref[...]) * q_scale).astype(BF16)
    ko_ref[0] = proc(k_ref[0].astype(F32), gk_ref[...]).astype(BF16)


def qk_prep(p3, col_q, col_k, gq, gk, cos, sin, do_norm, do_rope, q_scale, col_v=None):
    B, T, _ = p3.shape
    tm = min(GATTN_TK, T)
    QW = N_QHEADS * HEAD_DIM
    fixed = lambda b, i: (0, 0)
    tile = lambda col: pl.BlockSpec((1, tm, LANES), lambda b, i: (b, i, col // LANES))
    with_vt = col_v is not None
    in_specs = [pl.BlockSpec((1, tm, QW), lambda b, i: (b, i, col_q // QW)), tile(col_k)]
    args = [p3, p3]
    out_specs = [pl.BlockSpec((1, tm, QW), lambda b, i: (b, i, 0)), pl.BlockSpec((1, tm, LANES), lambda b, i: (b, i, 0))]
    out_shape = [jax.ShapeDtypeStruct((B, T, QW), BF16), jax.ShapeDtypeStruct((B, T, LANES), BF16)]
    if with_vt:
        in_specs.append(tile(col_v))
        args.append(p3)
        out_specs.append(pl.BlockSpec((1, 1, VT_ROWS, tm), lambda b, i: (b, i, 0, 0)))
        out_shape.append(jax.ShapeDtypeStruct((B, T // tm, VT_ROWS, tm), BF16))
    in_specs += [pl.BlockSpec((1, LANES), fixed), pl.BlockSpec((1, LANES), fixed),
                 pl.BlockSpec((tm, LANES), lambda b, i: (i, 0)), pl.BlockSpec((tm, LANES), lambda b, i: (i, 0))]
    args += [gq, gk, cos, sin]
    return pl.pallas_call(
        functools.partial(_qkprep_kernel, do_norm=do_norm, do_rope=do_rope, q_scale=q_scale, with_vt=with_vt),
        grid=(B, T // tm),
        in_specs=in_specs, out_specs=out_specs, out_shape=out_shape,
        compiler_params=_cparams(("arbitrary", "arbitrary")),
        name="qk_prep",
    )(*args)


def _stack_pair(q):
    lo = _lo_mask(q.shape)
    zero = jnp.zeros_like(q)
    return jnp.concatenate([jnp.where(lo, q, zero), jnp.where(lo, zero, q)], axis=0)


def _unstack_pair(o, tq):
    return jnp.where(_lo_mask((tq, LANES)), o[:tq], o[tq:])


def _nt_dot(a, b):
    return lax.dot_general(a, b, (((1,), (1,)), ((), ())), preferred_element_type=F32)


GATTN_TQ = 256
GATTN_TK = 512
LOG2E = 1.4426950408889634


def _gattn_kernel(q_ref, k_ref, vt_ref, kc_ref, vtc_ref, o_ref, s_scr):
    tq = q_ref.shape[1]
    n_pairs = q_ref.shape[2] // LANES
    C = kc_ref.shape[1]
    qqs = [_stack_pair(q_ref[0, :, p * LANES:(p + 1) * LANES]) for p in range(n_pairs)]

    def scores(kb, slot, n):
        for p, qq in enumerate(qqs):
            s_scr[slot, p, 0:n, :] = _nt_dot(kb, qq)

    def update(carries, slot, n, vtb):
        out = []
        for p, (m, acc) in enumerate(carries):
            m_new = jnp.maximum(m, jnp.max(s_scr[slot, p, 0:n, :], axis=0, keepdims=True))
            alpha = jnp.exp2(m - m_new)
            e = jnp.exp2((s_scr[slot, p, 0:n, :] - m_new).astype(BF16))
            out.append((m_new, alpha * acc + jnp.dot(vtb, e, preferred_element_type=F32)))
        return tuple(out)

    def k_chunk(i):
        return k_ref[0, pl.ds(pl.multiple_of(i * GATTN_TK, GATTN_TK), GATTN_TK), :]

    n_chunks = vt_ref.shape[1]
    carries = tuple((jnp.full((1, 2 * tq), NEG, F32), jnp.zeros((VT_ROWS, 2 * tq), F32)) for _ in range(n_pairs))
    scores(kc_ref[0], 1, C)
    scores(k_chunk(0), 0, GATTN_TK)
    carries = update(carries, 1, C, vtc_ref[0, 0])

    def body(j, carries):
        scores(k_chunk(2 * j + 1), 1, GATTN_TK)
        carries = update(carries, 0, GATTN_TK, vt_ref[0, 2 * j])
        scores(k_chunk(2 * j + 2), 0, GATTN_TK)
        return update(carries, 1, GATTN_TK, vt_ref[0, 2 * j + 1])

    assert n_chunks % 2 == 0
    carries = lax.fori_loop(0, n_chunks // 2 - 1, body, carries)
    scores(k_chunk(n_chunks - 1), 1, GATTN_TK)
    carries = update(carries, 0, GATTN_TK, vt_ref[0, n_chunks - 2])
    carries = update(carries, 1, GATTN_TK, vt_ref[0, n_chunks - 1])
    for p, (m, acc) in enumerate(carries):
        o_t = acc[0:LANES, :] / acc[LANES:LANES + 1, :]
        o_ref[0, :, p * LANES:(p + 1) * LANES] = _unstack_pair(o_t.T, tq).astype(BF16)


def global_attention(q, k, vt, kc, vtc):
    B, S, QW = q.shape
    C = kc.shape[1]
    tq = GATTN_TQ
    nck = vt.shape[1]
    return pl.pallas_call(
        _gattn_kernel,
        grid=(B, S // tq),
        in_specs=[pl.BlockSpec((1, tq, QW), lambda b, i: (b, i, 0)),
                  pl.BlockSpec((1, S, LANES), lambda b, i: (b, 0, 0)),
                  pl.BlockSpec((1, nck, VT_ROWS, GATTN_TK), lambda b, i: (b, 0, 0, 0)),
                  pl.BlockSpec((1, C, LANES), lambda b, i: (b, 0, 0)),
                  pl.BlockSpec((1, 1, VT_ROWS, C), lambda b, i: (b, 0, 0, 0))],
        out_specs=pl.BlockSpec((1, tq, QW), lambda b, i: (b, i, 0)),
        out_shape=jax.ShapeDtypeStruct((B, S, QW), BF16),
        scratch_shapes=[pltpu.VMEM((2, QW // LANES, GATTN_TK, 2 * tq), F32)],
        compiler_params=_cparams(("arbitrary", "arbitrary")),
        name="global_attention",
    )(q, k, vt, kc, vtc)


def _wattn_kernel(sink_ref, q_ref, k_ref, v_ref, kc_ref, vc_ref, o_ref):
    tq = q_ref.shape[1]
    S = k_ref.shape[1]
    span = tq + 2 * WINDOW
    i = pl.program_id(1)
    start = pl.multiple_of(jnp.clip(i * tq - WINDOW, 0, S - span), WINDOW)
    kb = k_ref[0, pl.ds(start, span), :]
    vb = v_ref[0, pl.ds(start, span), :]
    kc = kc_ref[0]
    vc = vc_ref[0]
    qpos = i * tq + lax.broadcasted_iota(jnp.int32, (tq, span), 0)
    kpos = start + lax.broadcasted_iota(jnp.int32, (tq, span), 1)
    valid = jnp.abs(kpos - qpos) <= WINDOW
    valid2 = jnp.concatenate([valid, valid], axis=0)
    top = lax.broadcasted_iota(jnp.int32, (2 * tq, 1), 0) < tq
    for p in range(q_ref.shape[2] // LANES):
        sl = slice(p * LANES, (p + 1) * LANES)
        qq = _stack_pair(q_ref[0, :, sl])
        sink = jnp.where(top, sink_ref[PAIR_ORDER[2 * p]], sink_ref[PAIR_ORDER[2 * p + 1]])
        s_loc = jnp.where(valid2, _nt_dot(qq, kb), NEG)
        s_ctx = _nt_dot(qq, kc)
        m = jnp.maximum(jnp.maximum(jnp.max(s_loc, axis=1, keepdims=True), jnp.max(s_ctx, axis=1, keepdims=True)), sink)
        p_loc = jnp.exp(s_loc - m)
        p_ctx = jnp.exp(s_ctx - m)
        l = jnp.sum(p_loc, axis=1, keepdims=True) + jnp.sum(p_ctx, axis=1, keepdims=True) + jnp.exp(sink - m)
        o = (jnp.dot(p_loc.astype(BF16), vb, preferred_element_type=F32)
             + jnp.dot(p_ctx.astype(BF16), vc, preferred_element_type=F32)) / l
        o_ref[0, :, sl] = _unstack_pair(o, tq).astype(BF16)


def window_attention(sink, q, k, p3, p3c):
    B, S, QW = q.shape
    C = p3c.shape[1]
    tq = 256
    im = lambda j: (lambda b, i: (b, 0, j))
    return pl.pallas_call(
        _wattn_kernel,
        grid=(B, S // tq),
        in_specs=[pl.BlockSpec(memory_space=pltpu.SMEM),
                  pl.BlockSpec((1, tq, QW), lambda b, i: (b, i, 0)),
                  pl.BlockSpec((1, S, LANES), im(0)),
                  pl.BlockSpec((1, S, LANES), im(COL_WV // LANES)),
                  pl.BlockSpec((1, C, LANES), im(COL_WK // LANES)),
                  pl.BlockSpec((1, C, LANES), im(COL_WV // LANES))],
        out_specs=pl.BlockSpec((1, tq, QW), lambda b, i: (b, i, 0)),
        out_shape=jax.ShapeDtypeStruct((B, S, QW), BF16),
        compiler_params=_cparams(("arbitrary", "arbitrary")),
        name="window_attention",
    )(sink, q, k, p3, p3c, p3c)


NB_QROWS = 4
NB_KROWS = NB_QROWS + NB_ROWS


def _nattn_kernel(q_ref, k_ref, v_ref, kc_ref, vc_ref, tab_ref, o_ref):
    tq = q_ref.shape[1]
    rows = k_ref.shape[1] // GRID_W
    nk = NB_KROWS * GRID_W
    j = pl.program_id(1)
    start = pl.multiple_of(jnp.clip(NB_QROWS * j - NB_ROWS // 2, 0, rows - NB_KROWS) * GRID_W, GRID_W)
    for p in range(q_ref.shape[2] // LANES):
        sl = slice(p * LANES, (p + 1) * LANES)
        kb = k_ref[0, pl.ds(start, nk), sl]
        vb = v_ref[0, pl.ds(start, nk), sl]
        kc = kc_ref[0, :, sl]
        vc = vc_ref[0, :, sl]
        qq = _stack_pair(q_ref[0, :, sl] * (HEAD_DIM ** -0.5))
        s_loc = _nt_dot(qq, kb) + tab_ref[0, 2 * p:2 * p + 2].reshape(2 * tq, nk)
        s_ctx = _nt_dot(qq, kc)
        m = jnp.maximum(jnp.max(s_loc, axis=1, keepdims=True), jnp.max(s_ctx, axis=1, keepdims=True))
        p_loc = jnp.exp(s_loc - m)
        p_ctx = jnp.exp(s_ctx - m)
        l = jnp.sum(p_loc, axis=1, keepdims=True) + jnp.sum(p_ctx, axis=1, keepdims=True)
        o = (jnp.dot(p_loc.astype(BF16), vb, preferred_element_type=F32)
             + jnp.dot(p_ctx.astype(BF16), vc, preferred_element_type=F32)) / l
        o_ref[0, :, sl] = _unstack_pair(o, tq).astype(BF16)


def neighbourhood_bias_table(rel_bias, rows):
    nblk = rows // NB_QROWS
    H = rel_bias.shape[0]
    W = GRID_W
    pad = W - NB_COLS
    vp = jnp.pad(rel_bias.astype(F32), ((0, 0), (0, 0), (pad, pad)))
    toep = jnp.stack([vp[:, :, W - 1 - qc:2 * W - 1 - qc] for qc in range(W)], axis=2)
    qc = np.arange(W)[:, None]
    kc = np.arange(W)[None, :]
    c0 = np.clip(qc - NB_COLS // 2, 0, W - NB_COLS)
    toep = jnp.where(((kc >= c0) & (kc < c0 + NB_COLS))[None, None], toep, NEG)
    neg_block = jnp.full((H, W, W), NEG, F32)
    tabs = []
    for jrep in (0, 1, nblk - 1):
        kstart = int(np.clip(NB_QROWS * jrep - NB_ROWS // 2, 0, rows - NB_KROWS))
        q_rows = []
        for qr in range(NB_QROWS):
            r = NB_QROWS * jrep + qr
            r0 = int(np.clip(r - NB_ROWS // 2, 0, rows - NB_ROWS))
            blocks = []
            for kr in range(kstart, kstart + NB_KROWS):
                blocks.append(toep[:, kr - r + NB_ROWS - 1] if r0 <= kr < r0 + NB_ROWS else neg_block)
            q_rows.append(jnp.concatenate(blocks, axis=2))
        tabs.append(jnp.concatenate(q_rows, axis=1))
    return jnp.stack(tabs)


def neighbourhood_attention(p3, p3c, table):
    B, S, _ = p3.shape
    C = p3c.shape[1]
    QW = N_QHEADS * HEAD_DIM
    tq = NB_QROWS * GRID_W
    nblk = S // tq
    nk = NB_KROWS * GRID_W
    cls = lambda j: jnp.where(j == 0, 0, jnp.where(j == nblk - 1, 2, 1))
    im = lambda col: (lambda b, j: (b, 0, col // QW))
    return pl.pallas_call(
        _nattn_kernel,
        grid=(B, nblk),
        in_specs=[pl.BlockSpec((1, tq, QW), lambda b, j: (b, j, COL_NQ // QW)),
                  pl.BlockSpec((1, S, QW), im(COL_NK)), pl.BlockSpec((1, S, QW), im(COL_NV)),
                  pl.BlockSpec((1, C, QW), im(COL_NK)), pl.BlockSpec((1, C, QW), im(COL_NV)),
                  pl.BlockSpec((1, N_QHEADS, tq, nk), lambda b, j: (cls(j), 0, 0, 0))],
        out_specs=pl.BlockSpec((1, tq, QW), lambda b, j: (b, j, 0)),
        out_shape=jax.ShapeDtypeStruct((B, S, QW), BF16),
        compiler_params=_cparams(("arbitrary", "arbitrary")),
        name="neighbourhood_attention",
    )(p3, p3, p3, p3c, p3c, table)


def _cattn_kernel(sink_ref, q_ref, k_ref, v_ref, o_ref, *, kv_tiles, use_sink, q_scale):
    C = q_ref.shape[1]
    top = lax.broadcasted_iota(jnp.int32, (2 * C, 1), 0) < C
    for p in range(q_ref.shape[2] // LANES):
        sl = slice(p * LANES, (p + 1) * LANES)
        ksl = sl if kv_tiles > 1 else slice(0, LANES)
        q = q_ref[0, :, sl]
        if q_scale != 1.0:
            q = q * q_scale
        s = _nt_dot(_stack_pair(q), k_ref[0, :, ksl])
        m = jnp.max(s, axis=1, keepdims=True)
        if use_sink:
            sink = jnp.where(top, sink_ref[PAIR_ORDER[2 * p]], sink_ref[PAIR_ORDER[2 * p + 1]])
            m = jnp.maximum(m, sink)
        e = jnp.exp(s - m)
        l = jnp.sum(e, axis=1, keepdims=True)
        if use_sink:
            l = l + jnp.exp(sink - m)
        o = jnp.dot(e.astype(BF16), v_ref[0, :, ksl], preferred_element_type=F32) / l
        o_ref[0, :, sl] = _unstack_pair(o, C).astype(BF16)


def context_attention(sink, q_arr, q_col, k_arr, k_col, v_arr, v_col, kv_tiles, use_sink, q_scale):
    B, C, _ = q_arr.shape
    QW = N_QHEADS * HEAD_DIM
    KW = kv_tiles * LANES
    return pl.pallas_call(
        functools.partial(_cattn_kernel, kv_tiles=kv_tiles, use_sink=use_sink, q_scale=q_scale),
        grid=(B,),
        in_specs=[pl.BlockSpec(memory_space=pltpu.SMEM),
                  pl.BlockSpec((1, C, QW), lambda b: (b, 0, q_col // QW)),
                  pl.BlockSpec((1, C, KW), lambda b: (b, 0, k_col // KW)),
                  pl.BlockSpec((1, C, KW), lambda b: (b, 0, v_col // KW))],
        out_specs=pl.BlockSpec((1, C, QW), lambda b: (b, 0, 0)),
        out_shape=jax.ShapeDtypeStruct((B, C, QW), BF16),
        compiler_params=_cparams(("arbitrary",)),
        name="context_attention",
    )(sink, q_arr, k_arr, v_arr)


def _merge_kernel(h0_ref, h1_ref, mo_ref, ng_ref, ab_ref, ac_ref, ad_ref, g0_ref, g1_ref, g2_ref, g3_ref,
                  wb_ref, wo_ref, x_ref, ga_ref, xo_ref):
    hs = h0_ref[...].astype(F32) + h1_ref[...].astype(F32)
    parts = []
    for h in range(M_HEADS):
        hh = hs[:, h * M_HEAD_DIM:(h + 1) * M_HEAD_DIM]
        parts.append(hh * lax.rsqrt(jnp.mean(hh * hh, axis=1, keepdims=True) + EPS))
    a0 = (jnp.concatenate(parts, axis=1) * ng_ref[...]) * _sigmoid(mo_ref[...].astype(F32))
    branches = (a0.astype(BF16), ab_ref[...], ac_ref[...], ad_ref[...])
    gates = (g0_ref, g1_ref, g2_ref, g3_ref)
    y = None
    for i in range(N_BRANCH):
        t = _sigmoid(gates[i][...].astype(F32)) * jnp.dot(branches[i], wb_ref[i], preferred_element_type=F32)
        y = t if y is None else y + t
    out = jnp.dot(y.astype(BF16), wo_ref[...], preferred_element_type=F32)
    xo_ref[...] = x_ref[...] + ga_ref[0] * out


def merge_branches(h0, h1, p2, norm_g, att_b, att_c, att_d, w_branch, w_out, x, ga, rows_per_mod):
    R, D = x.shape
    tm = min(512, R)
    tpm = rows_per_mod // tm
    row = lambda i: (i, 0)
    fixed = lambda i: (0, 0)
    gate_spec = lambda k: pl.BlockSpec((tm, D), lambda i: (i, COL_GATE // D + k))
    return pl.pallas_call(
        _merge_kernel,
        grid=(R // tm,),
        in_specs=[pl.BlockSpec((tm, M_WIDTH), row), pl.BlockSpec((tm, M_WIDTH), row),
                  pl.BlockSpec((tm, M_WIDTH), lambda i: (i, COL_MO // M_WIDTH)),
                  pl.BlockSpec((1, M_WIDTH), fixed),
                  pl.BlockSpec((tm, BRANCH_WIDTH), row), pl.BlockSpec((tm, BRANCH_WIDTH), row),
                  pl.BlockSpec((tm, BRANCH_WIDTH), row),
                  gate_spec(0), gate_spec(1), gate_spec(2), gate_spec(3),
                  pl.BlockSpec((N_BRANCH, BRANCH_WIDTH, D), lambda i: (0, 0, 0)),
                  pl.BlockSpec((D, D), fixed),
                  pl.BlockSpec((tm, D), row),
                  pl.BlockSpec((1, 1, D), lambda i: (i // tpm, 0, 0))],
        out_specs=pl.BlockSpec((tm, D), row),
        out_shape=jax.ShapeDtypeStruct((R, D), F32),
        compiler_params=_cparams(("arbitrary",)),
        name="merge_branches",
    )(h0, h1, p2, norm_g.reshape(1, -1), att_b, att_c, att_d, p2, p2, p2, p2, w_branch, w_out, x, ga)


def _router_kernel(x_ref, g_ref, sc_ref, sh_ref, wr_ref, rb_ref, h_ref, e_ref, w_ref):
    x = x_ref[...]
    y = x * lax.rsqrt(jnp.mean(x * x, axis=-1, keepdims=True) + EPS)
    h = (y * g_ref[...]) * (1.0 + sc_ref[0]) + sh_ref[0]
    _to_token_tiles(h_ref, h, x.shape[0])
    logits = lax.dot_general(wr_ref[...], h, (((1,), (1,)), ((), ())), precision=HIGHEST,
                             preferred_element_type=F32)
    scores = _sigmoid(logits)
    sel = scores + rb_ref[...]
    E = EXPERTS_PER_GROUP
    tm = x.shape[0]
    sub = lax.broadcasted_iota(jnp.int32, (E, tm), 0)
    best = None
    for g in range(N_GROUPS):
        v = sel[g * E:(g + 1) * E]
        sc = scores[g * E:(g + 1) * E]
        m1 = jnp.max(v, axis=0, keepdims=True)
        i1 = jnp.min(jnp.where(v == m1, sub, E), axis=0, keepdims=True)
        rest = jnp.where(sub == i1, -jnp.inf, v)
        m2 = jnp.max(rest, axis=0, keepdims=True)
        i2 = jnp.min(jnp.where(rest == m2, sub, E), axis=0, keepdims=True)
        s1 = jnp.sum(jnp.where(sub == i1, sc, 0.0), axis=0, keepdims=True)
        s2 = jnp.sum(jnp.where(sub == i2, sc, 0.0), axis=0, keepdims=True)
        cand = (m1 + m2, g * E + i1, g * E + i2, s1, s2)
        if best is None:
            best = cand
        else:
            take = cand[0] > best[0]
            best = tuple(jnp.where(take, cn, bs) for cn, bs in zip(cand, best))
    _, e1, e2, s1, s2 = best
    tot = s1 + s2
    e_ref[0:1, :] = e1
    e_ref[1:2, :] = e2
    w_ref[0:1, :] = s1 / tot
    w_ref[1:2, :] = s2 / tot


def router(x, g, sc, sh, w_router_t, router_bias, rows_per_mod):
    R, D = x.shape
    tm = min(512, R)
    tpm = rows_per_mod // tm
    mod = lambda i: (i // tpm, 0, 0)
    fixed = lambda i: (0, 0)
    return pl.pallas_call(
        _router_kernel,
        grid=(R // tm,),
        in_specs=[pl.BlockSpec((tm, D), lambda i: (i, 0)), pl.BlockSpec((1, D), fixed),
                  pl.BlockSpec((1, 1, D), mod), pl.BlockSpec((1, 1, D), mod),
                  pl.BlockSpec((N_EXPERTS, D), fixed), pl.BlockSpec((N_EXPERTS, 1), fixed)],
        out_specs=[pl.BlockSpec((tm * ROW_TILE, LANES), lambda i: (i, 0)), pl.BlockSpec((2, tm), lambda i: (0, i)),
                   pl.BlockSpec((2, tm), lambda i: (0, i))],
        out_shape=[jax.ShapeDtypeStruct((R * ROW_TILE, LANES), F32), jax.ShapeDtypeStruct((2, R), jnp.int32),
                   jax.ShapeDtypeStruct((2, R), F32)],
        compiler_params=_cparams(("arbitrary",)),
        name="router",
    )(x, g.reshape(1, D), sc, sh, w_router_t, router_bias.reshape(-1, 1))


N_XBUF = 4


def _expert_kernel(be_ref, nu_ref, tok_ref, nxt_ref, nx2_ref, dst_ref, h_hbm, wg_ref, wu_ref, wd_ref, y_hbm,
                   xbuf, ybuf, wgb, wub, wdb, gsem, ssem):
    i = pl.program_id(0)
    n_used = nu_ref[0]
    MB = MOE_BLOCK
    RT = ROW_TILE

    def gather_copy(t8, r, b):
        return pltpu.make_async_copy(h_hbm.at[pl.ds(pl.multiple_of(t8, RT), RT)], xbuf.at[b, pl.ds(r * RT, RT)],
                                     gsem.at[b])

    def scatter_copy(d8, r, b):
        return pltpu.make_async_copy(ybuf.at[b, pl.ds(r * RT, RT)], y_hbm.at[pl.ds(pl.multiple_of(d8, RT), RT)],
                                     ssem.at[b])

    def all_gathers(b):
        return pltpu.make_async_copy(h_hbm.at[pl.ds(0, MB * RT)], xbuf.at[b], gsem.at[b])

    def all_scatters(b):
        return pltpu.make_async_copy(ybuf.at[b], y_hbm.at[pl.ds(0, MB * RT)], ssem.at[b])

    def start_gather(idx_ref, b):
        for r in range(MB):
            gather_copy(idx_ref[0, 0, r], r, b).start(priority=r % 2)

    @pl.when(i == 0)
    def _():
        n_real = y_hbm.shape[0] - 2 * MB * RT
        ybuf[...] = jnp.zeros_like(ybuf)
        for b in range(2):
            fill = pltpu.make_async_copy(ybuf.at[b], y_hbm.at[pl.ds(n_real + b * MB * RT, MB * RT)], ssem.at[b])
            fill.start()
            fill.wait()

    @pl.when(jnp.logical_and(i == 0, n_used > 0))
    def _():
        start_gather(tok_ref, 0)
        start_gather(nxt_ref, 1)

    for k in range(N_XBUF):
        @pl.when(i % N_XBUF == k)
        def _(k=k):
            xb, yb = k, k % 2

            @pl.when(jnp.logical_and(jnp.logical_and(i >= n_used, i < n_used + 2), n_used > 0))
            def _():
                all_gathers(xb).wait()

            @pl.when(jnp.logical_and(i >= 2, i - 2 < n_used))
            def _():
                all_scatters(yb).wait()

            @pl.when(i < n_used)
            def _():
                prev = be_ref[jnp.maximum(i - 1, 0)]

                @pl.when(jnp.logical_or(i == 0, be_ref[i] != prev))
                def _():
                    wgb[...] = wg_ref[0, 0].astype(BF16)
                    wub[...] = wu_ref[0, 0].astype(BF16)
                    wdb[...] = wd_ref[0, 0].astype(BF16)

                all_gathers(xb).wait()
                x = _from_token_tiles(xbuf.at[xb], MB).astype(BF16)
                g = jnp.dot(x, wgb[...], preferred_element_type=F32)
                u = jnp.dot(x, wub[...], preferred_element_type=F32)
                hmid = ((g * _sigmoid(g)) * u).astype(BF16)
                start_gather(nx2_ref, (k + 2) % N_XBUF)
                _to_token_tiles(ybuf.at[yb], jnp.dot(hmid, wdb[...], preferred_element_type=F32), MB)
                for r in range(MB):
                    scatter_copy(dst_ref[0, 0, r], r, yb).start(priority=r % 2)


def expert_ffn(blk_e, n_used, slot_tok, slot_dst, h2, e_gate, e_up, e_down, layer, n_out_rows):
    nblk = slot_tok.shape[0]
    D = D_MODEL
    last = nblk - 1
    MB = MOE_BLOCK
    RT = ROW_TILE
    wmap = lambda i, be, nu: (layer, be[jnp.minimum(i, last)], 0, 0)
    smem_blk = lambda off: pl.BlockSpec((1, 1, MB), lambda i, be, nu: (jnp.minimum(i + off, last), 0, 0),
                                        memory_space=pltpu.SMEM)
    grid_spec = pltpu.PrefetchScalarGridSpec(
        num_scalar_prefetch=2,
        grid=(nblk + 2,),
        in_specs=[smem_blk(0), smem_blk(1), smem_blk(2), smem_blk(0),
                  pl.BlockSpec(memory_space=pl.ANY),
                  pl.BlockSpec((1, 1, D, D_EXPERT), wmap), pl.BlockSpec((1, 1, D, D_EXPERT), wmap),
                  pl.BlockSpec((1, 1, D_EXPERT, D), wmap)],
        out_specs=pl.BlockSpec(memory_space=pl.ANY),
        scratch_shapes=[pltpu.VMEM((N_XBUF, MB * RT, LANES), F32), pltpu.VMEM((2, MB * RT, LANES), F32),
                        pltpu.VMEM((D, D_EXPERT), BF16), pltpu.VMEM((D, D_EXPERT), BF16),
                        pltpu.VMEM((D_EXPERT, D), BF16),
                        pltpu.SemaphoreType.DMA((N_XBUF,)), pltpu.SemaphoreType.DMA((2,))],
    )
    return pl.pallas_call(
        _expert_kernel,
        grid_spec=grid_spec,
        out_shape=jax.ShapeDtypeStruct((n_out_rows * RT, LANES), F32),
        compiler_params=_cparams(("arbitrary",)),
        name="expert_ffn",
    )(blk_e, n_used, slot_tok, slot_tok, slot_tok, slot_dst, h2, e_gate, e_up, e_down)


def moe_ffn(h2, expert, e_gate, e_up, e_down, layer):
    T = expert.shape[1]
    MB = MOE_BLOCK
    n_assign = 2 * T
    e_flat = expert.reshape(-1)
    order = jnp.argsort(e_flat, stable=True).astype(jnp.int32)
    eids = jnp.arange(N_EXPERTS, dtype=jnp.int32)
    counts = jnp.sum((e_flat[:, None] == eids[None, :]).astype(jnp.int32), axis=0)
    starts = jnp.cumsum(counts) - counts
    padded = ((counts + MB - 1) // MB) * MB
    pends = jnp.cumsum(padded)
    pstarts = pends - padded
    nblk = -(-(n_assign + N_EXPERTS * (MB - 1)) // MB)
    cap = nblk * MB
    slot = jnp.arange(cap, dtype=jnp.int32)
    slot_e = jnp.sum((slot[:, None] >= pends[None, :]).astype(jnp.int32), axis=1)
    onehot = slot_e[:, None] == eids[None, :]
    pick = lambda v: jnp.sum(jnp.where(onehot, v[None, :], 0), axis=1)
    local = slot - pick(pstarts)
    real = jnp.logical_and(slot_e < N_EXPERTS, local < pick(counts))
    slot_assign = order[jnp.clip(pick(starts) + local, 0, n_assign - 1)]
    dump = n_assign + ((slot // MB) % 2) * MB + slot % MB
    slot_dst = (jnp.where(real, slot_assign, dump) * ROW_TILE).reshape(nblk, 1, MB)
    slot_tok = (jnp.where(real, slot_assign % T, 0) * ROW_TILE).reshape(nblk, 1, MB)
    blk_e = jnp.minimum(slot_e.reshape(nblk, MB)[:, 0], N_EXPERTS - 1)
    n_used = (pends[-1] // MB).astype(jnp.int32).reshape(1)
    return expert_ffn(blk_e, n_used, slot_tok, slot_dst, h2, e_gate, e_up, e_down, layer, n_assign + 2 * MB)


def _final_kernel(x_ref, y0_ref, y1_ref, wt_ref, ga_ref, g_ref, o_ref):
    x = x_ref[...] + ga_ref[0] * _moe_combine(y0_ref, y1_ref, wt_ref)
    o_ref[...] = (x * lax.rsqrt(jnp.mean(x * x, axis=-1, keepdims=True) + EPS)) * g_ref[...]


def final_norm(x, moe, ga, g, rows_per_mod):
    R, D = x.shape
    tm = min(1024, R)
    tpm = rows_per_mod // tm
    row = lambda i: (i, 0)
    moe_specs, moe_args = _moe_specs(moe, tm, 1)
    return pl.pallas_call(
        _final_kernel,
        grid=(R // tm,),
        in_specs=[pl.BlockSpec((tm, D), row)] + moe_specs
        + [pl.BlockSpec((1, 1, D), lambda i: (i // tpm, 0, 0)), pl.BlockSpec((1, D), lambda i: (0, 0))],
        out_specs=pl.BlockSpec((tm, D), row),
        out_shape=jax.ShapeDtypeStruct((R, D), F32),
        compiler_params=_cparams(("arbitrary",)),
        name="final_norm",
    )(x, *moe_args, ga, g.reshape(1, D))


_DEINTERLEAVE = np.concatenate([np.arange(0, HEAD_DIM, 2), np.arange(1, HEAD_DIM, 2)])


def _head_cols(n_heads, order, perm):
    return np.concatenate([h * HEAD_DIM + perm for h in order])


def _reorder_w_in(w_in):
    widths = (512, 512, 512, 512, 16, 512, 128, 128, 512, 128, 128, 512, 512, 512, 4096)
    offs = np.concatenate([[0], np.cumsum(widths)])
    (mq, mk, mv, mo, mif, gq, gk, gv, wq, wk, wv, nq, nk, nv, gate) = [
        w_in[:, offs[i]:offs[i + 1]] for i in range(len(widths))]
    D = w_in.shape[0]
    half = HEAD_DIM // 2

    def deinterleave(a, n_heads):
        return a.reshape(D, n_heads, half, 2).transpose(0, 1, 3, 2).reshape(D, n_heads * HEAD_DIM)

    def pair_heads(a):
        return a.reshape(D, 2, N_QHEADS // 2, HEAD_DIM).transpose(0, 2, 1, 3).reshape(D, N_QHEADS * HEAD_DIM)

    w = jnp.concatenate([mq, mk, mv, mo, pair_heads(deinterleave(gq, N_QHEADS)), pair_heads(deinterleave(wq, N_QHEADS)),
                         nq, nk, nv, deinterleave(gk, 2), gv, deinterleave(wk, 2), wv, gate], axis=1).astype(BF16)
    wif = jnp.pad(mif, ((0, 0), (0, LANES - mif.shape[1]))).astype(BF16)
    return w, wif


def _rope_tables(S):
    t = np.arange(S)
    n_freq = HEAD_DIM // 4
    inv_freq = jnp.asarray(ROPE_THETA, F32) ** (-jnp.arange(n_freq, dtype=F32) / n_freq)
    row = jnp.asarray(t // GRID_W, F32)
    col = jnp.asarray(t % GRID_W, F32)
    ang = jnp.concatenate([row[:, None] * inv_freq, col[:, None] * inv_freq], axis=-1)
    cos, sin = jnp.cos(ang), jnp.sin(ang)
    cos_t = jnp.tile(jnp.concatenate([cos, cos], axis=1), (1, 2))
    sin_t = jnp.tile(jnp.concatenate([-sin, sin], axis=1), (1, 2))
    return cos_t, sin_t


def kernel(x, c, ctx, c_ctx, w_mod, b_mod, norm1_g, norm2_g, w_in, m_conv_w, m_conv_b, m_gate_b, m_norm_g,
           g_qnorm, g_knorm, w_sink, n_rel_bias, w_branch, w_out, w_router, router_bias, e_gate, e_up, e_down,
           final_g):
    B, S, D = x.shape
    C = ctx.shape[1]
    L = w_mod.shape[0]
    R, RC = B * S, B * C
    rows = S // GRID_W

    c_all = jnp.zeros((16, D), F32).at[:B].set(c).at[B].set(c_ctx)
    mod = modulation(c_all, w_mod, b_mod).reshape(L, 16, 6, D)
    cos_t, sin_t = _rope_tables(S)
    w_router_t = w_router.T

    xl = x.reshape(R, D)
    xc = ctx.reshape(RC, D)
    moe_lat = moe_ctx = ga2_lat = ga2_ctx = None
    for l in range(L):
        need_ctx = l < L - 1
        lat = [mod[l, :B, i].reshape(B, 1, D) for i in range(6)]
        cx = [mod[l, B:B + 1, i].reshape(1, 1, D) for i in range(6)]
        w, wif = _reorder_w_in(w_in[l])
        gb = jnp.pad(m_gate_b[l].reshape(1, -1), ((0, 0), (0, LANES - 16)))
        gq = jnp.tile(g_qnorm[l][_DEINTERLEAVE], 2).reshape(1, LANES)
        gk = jnp.tile(g_knorm[l][_DEINTERLEAVE], 2).reshape(1, LANES)
        pair_rows = lambda a: a.reshape(2, N_QHEADS // 2, HEAD_DIM, D).transpose(1, 0, 2, 3).reshape(BRANCH_WIDTH, D)
        wb = jnp.stack([w_branch[l, 0], pair_rows(w_branch[l, 1]), pair_rows(w_branch[l, 2]),
                        w_branch[l, 3]]).astype(BF16)
        wo = w_out[l].astype(BF16)

        if l == 0:
            p_lat, g_lat = in_projection(xl, norm1_g[l], lat[1], lat[0], w, wif, gb, S)
            p_ctx, g_ctx = in_projection(xc, norm1_g[l], cx[1], cx[0], w, wif, gb, RC)
        else:
            xl, p_lat, g_lat = in_projection(xl, norm1_g[l], lat[1], lat[0], w, wif, gb, S, moe=moe_lat, ga=ga2_lat)
            xc, p_ctx, g_ctx = in_projection(xc, norm1_g[l], cx[1], cx[0], w, wif, gb, RC, moe=moe_ctx, ga=ga2_ctx)
        p3 = p_lat.reshape(B, S, -1)
        p3c = p_ctx.reshape(B, C, -1)

        qk_c = mlstm_conv(p3c, m_conv_w[l], m_conv_b[l])
        qk_l = mlstm_conv(p3, m_conv_w[l], m_conv_b[l])
        c0 = jnp.zeros((B, 2 * M_HEADS, MV_ROWS, M_HEAD_DIM), F32)
        m0 = jnp.full((B, 2 * M_HEADS, LANES), M_INIT, F32)
        hc0, hc1, c1, m1 = mlstm_scan(qk_c, mlstm_vt(p3c), g_ctx.reshape(B, C, LANES), c0, m0)
        hl0, hl1, _, _ = mlstm_scan(qk_l, mlstm_vt(p3), g_lat.reshape(B, S, LANES), c1, m1)

        scale = HEAD_DIM ** -0.5
        gq_l, gk_l, vt_l = qk_prep(p3, COL_GQ, COL_GK, gq, gk, cos_t, sin_t, True, True, scale * LOG2E, col_v=COL_GV)
        gq_c, gk_c, vt_c = qk_prep(p3c, COL_GQ, COL_GK, gq, gk, cos_t[:C], sin_t[:C], True, False, scale, col_v=COL_GV)
        att_b = global_attention(gq_l, gk_l, vt_l, gk_c, vt_c)

        wq_l, wk_l = qk_prep(p3, COL_WQ, COL_WK, gq, gk, cos_t, sin_t, False, True, scale)
        att_c = window_attention(w_sink[l], wq_l, wk_l, p3, p3c)

        table = neighbourhood_bias_table(n_rel_bias[l], rows)
        att_d = neighbourhood_attention(p3, p3c, table)

        xl = merge_branches(hl0.reshape(R, -1), hl1.reshape(R, -1), p_lat, m_norm_g[l], att_b.reshape(R, -1),
                            att_c.reshape(R, -1), att_d.reshape(R, -1), wb, wo, xl, lat[2], S)
        h2, ex, wt = router(xl, norm2_g[l], lat[4], lat[3], w_router_t, router_bias, S)
        if need_ctx:
            cb = context_attention(w_sink[l], gq_c, 0, gk_c, 0, p3c, COL_GV, 1, False, 1.0)
            cc = context_attention(w_sink[l], p3c, COL_WQ, p3c, COL_WK, p3c, COL_WV, 1, True, scale)
            cd = context_attention(w_sink[l], p3c, COL_NQ, p3c, COL_NK, p3c, COL_NV, 4, False, scale)
            xc = merge_branches(hc0.reshape(RC, -1), hc1.reshape(RC, -1), p_ctx, m_norm_g[l], cb.reshape(RC, -1),
                                cc.reshape(RC, -1), cd.reshape(RC, -1), wb, wo, xc, cx[2], RC)
            h2c, exc, wtc = router(xc, norm2_g[l], cx[4], cx[3], w_router_t, router_bias, RC)
            h2 = jnp.concatenate([h2, h2c], axis=0)
            ex = jnp.concatenate([ex, exc], axis=1)
            wt = jnp.concatenate([wt, wtc], axis=1)
        y = moe_ffn(h2, ex, e_gate, e_up, e_down, l)
        wt_t = wt.T
        moe_lat, ga2_lat = (y, wt_t, 0), lat[5]
        if need_ctx:
            moe_ctx, ga2_ctx = (y, wt_t, R), cx[5]
    out = final_norm(xl, moe_lat, ga2_lat, final_g, S)
    return out.reshape(B, S, D)
```

```python
import functools

import numpy as np
import jax
import jax.numpy as jnp
from jax import lax
from jax.experimental import pallas as pl
from jax.experimental.pallas import tpu as pltpu

F32 = jnp.float32
BF16 = jnp.bfloat16
HIGHEST = lax.Precision.HIGHEST

D_MODEL = 1024
DEPTH = 2
GRID_W = 64
HEAD_DIM = 64
ROPE_THETA = 10000.0
EPS = 1e-6
M_INIT = -1e30
NEG = -1e30
M_HEADS = 4
M_HEAD_DIM = 128
M_WIDTH = M_HEADS * M_HEAD_DIM
M_CHUNK = 128
N_QHEADS = 8
WINDOW = 128
NB_ROWS = 8
NB_COLS = 16
N_BRANCH = 4
BRANCH_WIDTH = 512
N_EXPERTS = 32
N_GROUPS = 4
EXPERTS_PER_GROUP = N_EXPERTS // N_GROUPS
D_EXPERT = 512
MOE_BLOCK = 256
LANES = 128
VMEM_LIMIT = 56 * 1024 * 1024

COL_MQ, COL_MK, COL_MV, COL_MO = 0, 512, 1024, 1536
COL_GQ, COL_WQ, COL_NQ, COL_NK, COL_NV = 2048, 2560, 3072, 3584, 4096
COL_GK, COL_GV, COL_WK, COL_WV = 4608, 4736, 4864, 4992
COL_GATE = 5120
N_PROJ_OUT = COL_GATE + N_BRANCH * D_MODEL
PAIR_ORDER = (0, 4, 1, 5, 2, 6, 3, 7)


def _cparams(sem):
    return pltpu.CompilerParams(dimension_semantics=sem, vmem_limit_bytes=VMEM_LIMIT)


def _sigmoid(x):
    return 1.0 / (1.0 + jnp.exp(-x))


def _lo_mask(shape):
    return lax.broadcasted_iota(jnp.int32, shape, len(shape) - 1) < HEAD_DIM


def _mod_kernel(c_ref, w_ref, b_ref, o_ref):
    c = c_ref[...]
    a = c * _sigmoid(c)
    o_ref[0] = jnp.dot(a, w_ref[0], precision=HIGHEST, preferred_element_type=F32) + b_ref[0]


def modulation(c_all, w_mod, b_mod):
    L, D, N = w_mod.shape
    R = c_all.shape[0]
    tn = 1536
    return pl.pallas_call(
        _mod_kernel,
        grid=(L, N // tn),
        in_specs=[pl.BlockSpec((R, D), lambda l, j: (0, 0)),
                  pl.BlockSpec((1, D, tn), lambda l, j: (l, 0, j)),
                  pl.BlockSpec((1, 1, tn), lambda l, j: (l, 0, j))],
        out_specs=pl.BlockSpec((1, R, tn), lambda l, j: (l, 0, j)),
        out_shape=jax.ShapeDtypeStruct((L, R, N), F32),
        compiler_params=_cparams(("arbitrary", "arbitrary")),
        name="modulation",
    )(c_all, w_mod, b_mod.reshape(L, 1, N))


def _log_sigmoid(z):
    return jnp.minimum(z, 0.0) - jnp.log(1.0 + jnp.exp(-jnp.abs(z)))


ROW_TILE = D_MODEL // LANES


def _to_token_tiles(ref, val, n):
    for j in range(ROW_TILE):
        ref[pl.ds(j, n, stride=ROW_TILE), :] = val[:, j * LANES:(j + 1) * LANES]


def _from_token_tiles(ref, n):
    return jnp.concatenate([ref[pl.ds(j, n, stride=ROW_TILE), :] for j in range(ROW_TILE)], axis=1)


def _moe_combine(y0_ref, y1_ref, wt_ref):
    wt = wt_ref[...]
    n = wt.shape[0]
    return wt[:, 0:1] * _from_token_tiles(y0_ref, n) + wt[:, 1:2] * _from_token_tiles(y1_ref, n)


def _inproj_kernel(*refs, has_f):
    if has_f:
        (x_ref, y0_ref, y1_ref, wt_ref, ga_ref, g_ref, sc_ref, sh_ref, w_ref, wif_ref, gb_ref,
         xo_ref, p_ref, gate_ref, h_scr) = refs
    else:
        x_ref, g_ref, sc_ref, sh_ref, w_ref, wif_ref, gb_ref, p_ref, gate_ref, h_scr = refs

    @pl.when(pl.program_id(1) == 0)
    def _():
        x = x_ref[...]
        if has_f:
            x = x + ga_ref[0] * _moe_combine(y0_ref, y1_ref, wt_ref)
            xo_ref[...] = x
        y = x * lax.rsqrt(jnp.mean(x * x, axis=-1, keepdims=True) + EPS)
        h = (y * g_ref[...]) * (1.0 + sc_ref[0]) + sh_ref[0]
        hb = h.astype(BF16)
        h_scr[...] = hb
        z = jnp.dot(hb, wif_ref[...], preferred_element_type=F32) + gb_ref[...]
        lane = lax.broadcasted_iota(jnp.int32, z.shape, 1)
        is_forget = (lane & 0xF4) == 4
        gate_ref[...] = jnp.where(is_forget, _log_sigmoid(z), z)

    p_ref[...] = jnp.dot(h_scr[...], w_ref[...], preferred_element_type=F32).astype(BF16)


def _moe_specs(moe, tm, nidx):
    y, wt, row0 = moe
    T = wt.shape[0]
    assert row0 % tm == 0 and T % tm == 0, (row0, T, tm)
    b0, b1 = row0 // tm, (T + row0) // tm
    if nidx == 2:
        maps = (lambda i, j: (b0 + i, 0), lambda i, j: (b1 + i, 0))
    else:
        maps = (lambda i: (b0 + i, 0), lambda i: (b1 + i, 0))
    tile_rows = pl.BlockSpec((tm * ROW_TILE, LANES), maps[0]), pl.BlockSpec((tm * ROW_TILE, LANES), maps[1])
    return [*tile_rows, pl.BlockSpec((tm, 2), maps[0])], [y, y, wt]


def in_projection(x, g, sc, sh, w, wif, gb, rows_per_mod, moe=None, ga=None):
    R, D = x.shape
    N = w.shape[1]
    tm = min(1024, R)
    tn = 1024
    tpm = rows_per_mod // tm
    has_f = moe is not None
    row = lambda i, j: (i, 0)
    mod = lambda i, j: (i // tpm, 0, 0)
    fixed = lambda i, j: (0, 0)
    in_specs = [pl.BlockSpec((tm, D), row)]
    args = [x]
    if has_f:
        moe_specs, moe_args = _moe_specs(moe, tm, 2)
        in_specs += moe_specs + [pl.BlockSpec((1, 1, D), mod)]
        args += moe_args + [ga]
    in_specs += [pl.BlockSpec((1, D), fixed), pl.BlockSpec((1, 1, D), mod), pl.BlockSpec((1, 1, D), mod),
                 pl.BlockSpec((D, tn), lambda i, j: (0, j)), pl.BlockSpec((D, LANES), fixed),
                 pl.BlockSpec((1, LANES), fixed)]
    args += [g.reshape(1, D), sc, sh, w, wif, gb]
    out_specs = [pl.BlockSpec((tm, tn), lambda i, j: (i, j)), pl.BlockSpec((tm, LANES), row)]
    out_shape = [jax.ShapeDtypeStruct((R, N), BF16), jax.ShapeDtypeStruct((R, LANES), F32)]
    if has_f:
        out_specs = [pl.BlockSpec((tm, D), row)] + out_specs
        out_shape = [jax.ShapeDtypeStruct((R, D), F32)] + out_shape
    return pl.pallas_call(
        functools.partial(_inproj_kernel, has_f=has_f),
        grid=(R // tm, N // tn),
        in_specs=in_specs, out_specs=out_specs, out_shape=out_shape,
        scratch_shapes=[pltpu.VMEM((tm, D), BF16)],
        compiler_params=_cparams(("arbitrary", "arbitrary")),
        name="in_projection",
    )(*args)


def _mconv_kernel(x_ref, w_ref, b_ref, o_ref, *, k_scale):
    x = x_ref[0].astype(F32)
    T = x.shape[0]
    row = lax.broadcasted_iota(jnp.int32, x.shape, 0)
    x_prev = jnp.where(row == 0, 0.0, pltpu.roll(x, 1, 0))
    x_next = jnp.where(row == T - 1, 0.0, pltpu.roll(x, T - 1, 0))
    y = x_prev * w_ref[0:1, :] + x * w_ref[1:2, :] + x_next * w_ref[2:3, :] + b_ref[...]
    y = y * _sigmoid(y)
    scale = jnp.where(pl.program_id(1) >= pl.num_programs(1) // 2, k_scale, 1.0)
    o_ref[0] = (y * scale).astype(BF16)


def mlstm_conv(p3, conv_w, conv_b):
    B, T, _ = p3.shape
    tc = 256
    nct = 2 * M_WIDTH // tc
    return pl.pallas_call(
        functools.partial(_mconv_kernel, k_scale=M_HEAD_DIM ** -0.5),
        grid=(B, nct),
        in_specs=[pl.BlockSpec((1, T, tc), lambda b, j: (b, 0, j)),
                  pl.BlockSpec((3, tc), lambda b, j: (0, j)),
                  pl.BlockSpec((1, tc), lambda b, j: (0, j))],
        out_specs=pl.BlockSpec((1, T, tc), lambda b, j: (b, 0, j)),
        out_shape=jax.ShapeDtypeStruct((B, T, 2 * M_WIDTH), BF16),
        compiler_params=_cparams(("arbitrary", "arbitrary")),
        name="mlstm_conv",
    )(p3, conv_w, conv_b.reshape(1, -1))


MLSTM_BT = 4
MV_ROWS = M_HEAD_DIM + 16


def _mlstm_vt_kernel(v_ref, vt_ref):
    ones = jnp.ones((MV_ROWS - M_HEAD_DIM, M_CHUNK), BF16)
    for c in range(vt_ref.shape[1]):
        for h in range(M_HEADS):
            v = v_ref[0, c * M_CHUNK:(c + 1) * M_CHUNK, h * M_HEAD_DIM:(h + 1) * M_HEAD_DIM]
            vt_ref[0, c, h, 0:M_HEAD_DIM, :] = v.astype(F32).T.astype(BF16)
            vt_ref[0, c, h, M_HEAD_DIM:MV_ROWS, :] = ones


def mlstm_vt(p3):
    B, T, _ = p3.shape
    nc = T // M_CHUNK
    cb = min(8, nc)
    return pl.pallas_call(
        _mlstm_vt_kernel,
        grid=(B, nc // cb),
        in_specs=[pl.BlockSpec((1, cb * M_CHUNK, M_WIDTH), lambda b, c: (b, c, COL_MV // M_WIDTH))],
        out_specs=pl.BlockSpec((1, cb, M_HEADS, MV_ROWS, M_CHUNK), lambda b, c: (b, c, 0, 0, 0)),
        out_shape=jax.ShapeDtypeStruct((B, nc, M_HEADS, MV_ROWS, M_CHUNK), BF16),
        compiler_params=_cparams(("arbitrary", "arbitrary")),
        name="mlstm_vt",
    )(p3)


def _mlstm_kernel(q0_ref, k0_ref, v0_ref, g0_ref, q1_ref, k1_ref, v1_ref, g1_ref, cin_ref, min_ref,
                  h0_ref, h1_ref, cout_ref, mout_ref, c_scr, m_scr):
    c = pl.program_id(1)
    L = M_CHUNK

    @pl.when(c == 0)
    def _():
        c_scr[...] = cin_ref[...]
        m_scr[...] = min_ref[...]

    row = lax.broadcasted_iota(jnp.int32, (L, L), 0)
    col = lax.broadcasted_iota(jnp.int32, (L, L), 1)
    all_ones = jnp.ones((L, L), F32)
    dirs = ((q0_ref, k0_ref, v0_ref, g0_ref, h0_ref), (q1_ref, k1_ref, v1_ref, g1_ref, h1_ref))
    for bi, (d, (q_ref, k_ref, v_ref, g_ref, h_ref)) in [(bi, dr) for bi in range(q0_ref.shape[0])
                                                         for dr in enumerate(dirs)]:
        G = g_ref[bi]
        tri = jnp.where(col <= row, 1.0, 0.0) if d == 0 else jnp.where(col >= row, 1.0, 0.0)
        CS = jnp.dot(tri, G, precision=HIGHEST, preferred_element_type=F32)
        TOT = jnp.dot(all_ones, G, precision=HIGHEST, preferred_element_type=F32)
        GT, CST, TOTT = G.T, CS.T, TOT.T
        keep = (row <= col) if d == 0 else (row >= col)
        for h in range(M_HEADS):
            ci, cf, sidx = d * 8 + h, d * 8 + 4 + h, d * M_HEADS + h
            b_row, i_row, bend = CST[cf:cf + 1, :], GT[ci:ci + 1, :], TOTT[cf:cf + 1, :]
            col_s = G[:, ci:ci + 1] - CS[:, cf:cf + 1]
            m_prev = m_scr[bi, sidx:sidx + 1, :]
            hs = slice(h * M_HEAD_DIM, (h + 1) * M_HEAD_DIM)
            q = q_ref[bi, :, hs]
            k = k_ref[bi, :, hs]
            vt = v_ref[bi, 0, h]
            ct = c_scr[bi, sidx]

            dmat = jnp.where(keep, b_row + col_s, NEG)
            m_inter = b_row + m_prev
            m_t = jnp.maximum(jnp.max(dmat, axis=0, keepdims=True), m_inter)
            a_t = _nt_dot(k, q) * jnp.exp(dmat - m_t)
            inter = jnp.exp(m_inter - m_t)
            nd = (jnp.dot(vt, a_t.astype(BF16), preferred_element_type=F32)
                  + inter * _nt_dot(ct.astype(BF16), q))
            den = nd[M_HEAD_DIM:M_HEAD_DIM + 1, :]
            h_t = nd[0:M_HEAD_DIM, :] / jnp.maximum(jnp.abs(den), jnp.exp(-m_t))
            h_ref[bi, :, hs] = h_t.T.astype(BF16)

            g_row = bend - b_row + i_row
            m_new = jnp.maximum(bend + m_prev, jnp.max(g_row, axis=1, keepdims=True))
            decay = jnp.exp(bend + m_prev - m_new)
            vw = (vt.astype(F32) * jnp.exp(g_row - m_new)).astype(BF16)
            c_scr[bi, sidx] = decay * ct + jnp.dot(vw, k, preferred_element_type=F32)
            m_scr[bi, sidx:sidx + 1, :] = m_new

    @pl.when(c == pl.num_programs(1) - 1)
    def _():
        cout_ref[...] = c_scr[...]
        mout_ref[...] = m_scr[...]


def mlstm_scan(qk, vt, gates, c_in, m_in):
    B, T, _ = qk.shape
    nc = T // M_CHUNK
    W = M_WIDTH
    bt = MLSTM_BT
    fwd = lambda j: (lambda b, c: (b, c, j))
    bwd = lambda j: (lambda b, c: (b, nc - 1 - c, j))
    st4 = lambda b, c: (b, 0, 0, 0)
    st3 = lambda b, c: (b, 0, 0)
    blk = lambda w, im: pl.BlockSpec((bt, M_CHUNK, w), im)
    vblk = lambda rev: pl.BlockSpec((bt, 1, M_HEADS, MV_ROWS, M_CHUNK),
                                    lambda b, c: (b, nc - 1 - c if rev else c, 0, 0, 0))
    state = pl.BlockSpec((bt, 2 * M_HEADS, MV_ROWS, M_HEAD_DIM), st4)
    in_specs = [blk(W, fwd(0)), blk(W, fwd(1)), vblk(False), blk(LANES, fwd(0)),
                blk(W, bwd(0)), blk(W, bwd(1)), vblk(True), blk(LANES, bwd(0)),
                state, pl.BlockSpec((bt, 2 * M_HEADS, LANES), st3)]
    out_specs = [blk(W, fwd(0)), blk(W, bwd(0)), state, pl.BlockSpec((bt, 2 * M_HEADS, LANES), st3)]
    out_shape = [jax.ShapeDtypeStruct((B, T, W), BF16), jax.ShapeDtypeStruct((B, T, W), BF16),
                 jax.ShapeDtypeStruct(c_in.shape, F32), jax.ShapeDtypeStruct(m_in.shape, F32)]
    return pl.pallas_call(
        _mlstm_kernel,
        grid=(B // bt, nc),
        in_specs=in_specs, out_specs=out_specs, out_shape=out_shape,
        scratch_shapes=[pltpu.VMEM((bt, 2 * M_HEADS, MV_ROWS, M_HEAD_DIM), F32),
                        pltpu.VMEM((bt, 2 * M_HEADS, LANES), F32)],
        compiler_params=_cparams(("arbitrary", "arbitrary")),
        name="mlstm_scan",
    )(qk, qk, vt, gates, qk, qk, vt, gates, c_in, m_in)


VT_ROWS = LANES + 16


def _qkprep_kernel(*refs, do_norm, do_rope, q_scale, with_vt):
    if with_vt:
        q_ref, k_ref, v_ref, gq_ref, gk_ref, cos_ref, sin_ref, qo_ref, ko_ref, vt_ref = refs
        vt_ref[0, 0, 0:LANES, :] = v_ref[0].astype(F32).T.astype(BF16)
        vt_ref[0, 0, LANES:VT_ROWS, :] = jnp.ones((VT_ROWS - LANES, v_ref.shape[1]), BF16)
    else:
        q_ref, k_ref, gq_ref, gk_ref, cos_ref, sin_ref, qo_ref, ko_ref = refs
    tm = q_ref.shape[1]
    lane = lax.broadcasted_iota(jnp.int32, (tm, LANES), 1)
    lo = lane < HEAD_DIM
    first_half = (lane & (HEAD_DIM // 2)) == 0

    def proc(x, g):
        if do_norm:
            ss = x * x
            s_lo = jnp.sum(jnp.where(lo, ss, 0.0), axis=1, keepdims=True)
            s_hi = jnp.sum(jnp.where(lo, 0.0, ss), axis=1, keepdims=True)
            x = (x * lax.rsqrt(jnp.where(lo, s_lo, s_hi) / HEAD_DIM + EPS)) * g
        if do_rope:
            partner = jnp.where(first_half, pltpu.roll(x, LANES - HEAD_DIM // 2, 1), pltpu.roll(x, HEAD_DIM // 2, 1))
            x = x * cos_ref[...] + partner * sin_ref[...]
        return x

    for p in range(q_ref.shape[2] // LANES):
        sl = slice(p * LANES, (p + 1) * LANES)
        qo_ref[0, :, sl] = (proc(q_ref[0, :, sl].astype(F32), gq_ref[...]) * q_scale).astype(BF16)
    ko_ref[0] = proc(k_ref[0].astype(F32), gk_ref[...]).astype(BF16)


def qk_prep(p3, col_q, col_k, gq, gk, cos, sin, do_norm, do_rope, q_scale, col_v=None):
    B, T, _ = p3.shape
    tm = min(GATTN_TK, T)
    QW = N_QHEADS * HEAD_DIM
    fixed = lambda b, i: (0, 0)
    tile = lambda col: pl.BlockSpec((1, tm, LANES), lambda b, i: (b, i, col // LANES))
    with_vt = col_v is not None
    in_specs = [pl.BlockSpec((1, tm, QW), lambda b, i: (b, i, col_q // QW)), tile(col_k)]
    args = [p3, p3]
    out_specs = [pl.BlockSpec((1, tm, QW), lambda b, i: (b, i, 0)), pl.BlockSpec((1, tm, LANES), lambda b, i: (b, i, 0))]
    out_shape = [jax.ShapeDtypeStruct((B, T, QW), BF16), jax.ShapeDtypeStruct((B, T, LANES), BF16)]
    if with_vt:
        in_specs.append(tile(col_v))
        args.append(p3)
        out_specs.append(pl.BlockSpec((1, 1, VT_ROWS, tm), lambda b, i: (b, i, 0, 0)))
        out_shape.append(jax.ShapeDtypeStruct((B, T // tm, VT_ROWS, tm), BF16))
    in_specs += [pl.BlockSpec((1, LANES), fixed), pl.BlockSpec((1, LANES), fixed),
                 pl.BlockSpec((tm, LANES), lambda b, i: (i, 0)), pl.BlockSpec((tm, LANES), lambda b, i: (i, 0))]
    args += [gq, gk, cos, sin]
    return pl.pallas_call(
        functools.partial(_qkprep_kernel, do_norm=do_norm, do_rope=do_rope, q_scale=q_scale, with_vt=with_vt),
        grid=(B, T // tm),
        in_specs=in_specs, out_specs=out_specs, out_shape=out_shape,
        compiler_params=_cparams(("arbitrary", "arbitrary")),
        name="qk_prep",
    )(*args)


def _stack_pair(q):
    lo = _lo_mask(q.shape)
    zero = jnp.zeros_like(q)
    return jnp.concatenate([jnp.where(lo, q, zero), jnp.where(lo, zero, q)], axis=0)


def _unstack_pair(o, tq):
    return jnp.where(_lo_mask((tq, LANES)), o[:tq], o[tq:])


def _nt_dot(a, b):
    return lax.dot_general(a, b, (((1,), (1,)), ((), ())), preferred_element_type=F32)


GATTN_TQ = 256
GATTN_TK = 512
LOG2E = 1.4426950408889634


def _gattn_kernel(q_ref, k_ref, vt_ref, kc_ref, vtc_ref, o_ref, s_scr, mx_scr):
    tq = q_ref.shape[1]
    n_pairs = q_ref.shape[2] // LANES
    C = kc_ref.shape[1]
    qqs = [_stack_pair(q_ref[0, :, p * LANES:(p + 1) * LANES]) for p in range(n_pairs)]

    def scores(kb, slot, n):
        for p, qq in enumerate(qqs):
            s = _nt_dot(kb, qq)
            s_scr[slot, p, 0:n, :] = s
            mx_scr[slot, p] = jnp.max(s, axis=0, keepdims=True)

    def update(carries, slot, n, vtb):
        out = []
        for p, (m, acc) in enumerate(carries):
            m_new = jnp.maximum(m, mx_scr[slot, p])
            alpha = jnp.exp2(m - m_new)
            e = jnp.exp2((s_scr[slot, p, 0:n, :] - m_new).astype(BF16))
            out.append((m_new, alpha * acc + jnp.dot(vtb, e, preferred_element_type=F32)))
        return tuple(out)

    def k_chunk(i):
        return k_ref[0, pl.ds(pl.multiple_of(i * GATTN_TK, GATTN_TK), GATTN_TK), :]

    n_chunks = vt_ref.shape[1]
    carries = tuple((jnp.full((1, 2 * tq), NEG, F32), jnp.zeros((VT_ROWS, 2 * tq), F32)) for _ in range(n_pairs))
    scores(kc_ref[0], 1, C)
    scores(k_chunk(0), 0, GATTN_TK)
    carries = update(carries, 1, C, vtc_ref[0, 0])

    def body(j, carries):
        scores(k_chunk(2 * j + 1), 1, GATTN_TK)
        carries = update(carries, 0, GATTN_TK, vt_ref[0, 2 * j])
        scores(k_chunk(2 * j + 2), 0, GATTN_TK)
        return update(carries, 1, GATTN_TK, vt_ref[0, 2 * j + 1])

    assert n_chunks % 2 == 0
    carries = lax.fori_loop(0, n_chunks // 2 - 1, body, carries)
    scores(k_chunk(n_chunks - 1), 1, GATTN_TK)
    carries = update(carries, 0, GATTN_TK, vt_ref[0, n_chunks - 2])
    carries = update(carries, 1, GATTN_TK, vt_ref[0, n_chunks - 1])
    for p, (m, acc) in enumerate(carries):
        o_t = acc[0:LANES, :] / acc[LANES:LANES + 1, :]
        o_ref[0, :, p * LANES:(p + 1) * LANES] = _unstack_pair(o_t.T, tq).astype(BF16)


def global_attention(q, k, vt, kc, vtc):
    B, S, QW = q.shape
    C = kc.shape[1]
    tq = GATTN_TQ
    nck = vt.shape[1]
    return pl.pallas_call(
        _gattn_kernel,
        grid=(B, S // tq),
        in_specs=[pl.BlockSpec((1, tq, QW), lambda b, i: (b, i, 0)),
                  pl.BlockSpec((1, S, LANES), lambda b, i: (b, 0, 0)),
                  pl.BlockSpec((1, nck, VT_ROWS, GATTN_TK), lambda b, i: (b, 0, 0, 0)),
                  pl.BlockSpec((1, C, LANES), lambda b, i: (b, 0, 0)),
                  pl.BlockSpec((1, 1, VT_ROWS, C), lambda b, i: (b, 0, 0, 0))],
        out_specs=pl.BlockSpec((1, tq, QW), lambda b, i: (b, i, 0)),
        out_shape=jax.ShapeDtypeStruct((B, S, QW), BF16),
        scratch_shapes=[pltpu.VMEM((2, QW // LANES, GATTN_TK, 2 * tq), F32),
                        pltpu.VMEM((2, QW // LANES, 1, 2 * tq), F32)],
        compiler_params=_cparams(("arbitrary", "arbitrary")),
        name="global_attention",
    )(q, k, vt, kc, vtc)


def _wattn_kernel(sink_ref, q_ref, k_ref, v_ref, kc_ref, vc_ref, o_ref):
    tq = q_ref.shape[1]
    S = k_ref.shape[1]
    span = tq + 2 * WINDOW
    i = pl.program_id(1)
    start = pl.multiple_of(jnp.clip(i * tq - WINDOW, 0, S - span), WINDOW)
    kb = k_ref[0, pl.ds(start, span), :]
    vb = v_ref[0, pl.ds(start, span), :]
    kc = kc_ref[0]
    vc = vc_ref[0]
    qpos = i * tq + lax.broadcasted_iota(jnp.int32, (tq, span), 0)
    kpos = start + lax.broadcasted_iota(jnp.int32, (tq, span), 1)
    valid = jnp.abs(kpos - qpos) <= WINDOW
    valid2 = jnp.concatenate([valid, valid], axis=0)
    top = lax.broadcasted_iota(jnp.int32, (2 * tq, 1), 0) < tq
    for p in range(q_ref.shape[2] // LANES):
        sl = slice(p * LANES, (p + 1) * LANES)
        qq = _stack_pair(q_ref[0, :, sl])
        sink = jnp.where(top, sink_ref[PAIR_ORDER[2 * p]], sink_ref[PAIR_ORDER[2 * p + 1]])
        s_loc = jnp.where(valid2, _nt_dot(qq, kb), NEG)
        s_ctx = _nt_dot(qq, kc)
        m = jnp.maximum(jnp.maximum(jnp.max(s_loc, axis=1, keepdims=True), jnp.max(s_ctx, axis=1, keepdims=True)), sink)
        p_loc = jnp.exp(s_loc - m)
        p_ctx = jnp.exp(s_ctx - m)
        l = jnp.sum(p_loc, axis=1, keepdims=True) + jnp.sum(p_ctx, axis=1, keepdims=True) + jnp.exp(sink - m)
        o = (jnp.dot(p_loc.astype(BF16), vb, preferred_element_type=F32)
             + jnp.dot(p_ctx.astype(BF16), vc, preferred_element_type=F32)) / l
        o_ref[0, :, sl] = _unstack_pair(o, tq).astype(BF16)


def window_attention(sink, q, k, p3, p3c):
    B, S, QW = q.shape
    C = p3c.shape[1]
    tq = 256
    im = lambda j: (lambda b, i: (b, 0, j))
    return pl.pallas_call(
        _wattn_kernel,
        grid=(B, S // tq),
        in_specs=[pl.BlockSpec(memory_space=pltpu.SMEM),
                  pl.BlockSpec((1, tq, QW), lambda b, i: (b, i, 0)),
                  pl.BlockSpec((1, S, LANES), im(0)),
                  pl.BlockSpec((1, S, LANES), im(COL_WV // LANES)),
                  pl.BlockSpec((1, C, LANES), im(COL_WK // LANES)),
                  pl.BlockSpec((1, C, LANES), im(COL_WV // LANES))],
        out_specs=pl.BlockSpec((1, tq, QW), lambda b, i: (b, i, 0)),
        out_shape=jax.ShapeDtypeStruct((B, S, QW), BF16),
        compiler_params=_cparams(("arbitrary", "arbitrary")),
        name="window_attention",
    )(sink, q, k, p3, p3c, p3c)


NB_QROWS = 4
NB_KROWS = NB_QROWS + NB_ROWS


def _nattn_kernel(q_ref, k_ref, v_ref, kc_ref, vc_ref, tab_ref, o_ref):
    tq = q_ref.shape[1]
    rows = k_ref.shape[1] // GRID_W
    nk = NB_KROWS * GRID_W
    j = pl.program_id(1)
    start = pl.multiple_of(jnp.clip(NB_QROWS * j - NB_ROWS // 2, 0, rows - NB_KROWS) * GRID_W, GRID_W)
    for p in range(q_ref.shape[2] // LANES):
        sl = slice(p * LANES, (p + 1) * LANES)
        kb = k_ref[0, pl.ds(start, nk), sl]
        vb = v_ref[0, pl.ds(start, nk), sl]
        kc = kc_ref[0, :, sl]
        vc = vc_ref[0, :, sl]
        qq = _stack_pair(q_ref[0, :, sl] * (HEAD_DIM ** -0.5))
        s_loc = _nt_dot(qq, kb) + tab_ref[0, 2 * p:2 * p + 2].reshape(2 * tq, nk)
        s_ctx = _nt_dot(qq, kc)
        m = jnp.maximum(jnp.max(s_loc, axis=1, keepdims=True), jnp.max(s_ctx, axis=1, keepdims=True))
        p_loc = jnp.exp(s_loc - m)
        p_ctx = jnp.exp(s_ctx - m)
        l = jnp.sum(p_loc, axis=1, keepdims=True) + jnp.sum(p_ctx, axis=1, keepdims=True)
        o = (jnp.dot(p_loc.astype(BF16), vb, preferred_element_type=F32)
             + jnp.dot(p_ctx.astype(BF16), vc, preferred_element_type=F32)) / l
        o_ref[0, :, sl] = _unstack_pair(o, tq).astype(BF16)


def neighbourhood_bias_table(rel_bias, rows):
    nblk = rows // NB_QROWS
    H = rel_bias.shape[0]
    W = GRID_W
    pad = W - NB_COLS
    vp = jnp.pad(rel_bias.astype(F32), ((0, 0), (0, 0), (pad, pad)))
    toep = jnp.stack([vp[:, :, W - 1 - qc:2 * W - 1 - qc] for qc in range(W)], axis=2)
    qc = np.arange(W)[:, None]
    kc = np.arange(W)[None, :]
    c0 = np.clip(qc - NB_COLS // 2, 0, W - NB_COLS)
    toep = jnp.where(((kc >= c0) & (kc < c0 + NB_COLS))[None, None], toep, NEG)
    neg_block = jnp.full((H, W, W), NEG, F32)
    tabs = []
    for jrep in (0, 1, nblk - 1):
        kstart = int(np.clip(NB_QROWS * jrep - NB_ROWS // 2, 0, rows - NB_KROWS))
        q_rows = []
        for qr in range(NB_QROWS):
            r = NB_QROWS * jrep + qr
            r0 = int(np.clip(r - NB_ROWS // 2, 0, rows - NB_ROWS))
            blocks = []
            for kr in range(kstart, kstart + NB_KROWS):
                blocks.append(toep[:, kr - r + NB_ROWS - 1] if r0 <= kr < r0 + NB_ROWS else neg_block)
            q_rows.append(jnp.concatenate(blocks, axis=2))
        tabs.append(jnp.concatenate(q_rows, axis=1))
    return jnp.stack(tabs)


def neighbourhood_attention(p3, p3c, table):
    B, S, _ = p3.shape
    C = p3c.shape[1]
    QW = N_QHEADS * HEAD_DIM
    tq = NB_QROWS * GRID_W
    nblk = S // tq
    nk = NB_KROWS * GRID_W
    cls = lambda j: jnp.where(j == 0, 0, jnp.where(j == nblk - 1, 2, 1))
    im = lambda col: (lambda b, j: (b, 0, col // QW))
    return pl.pallas_call(
        _nattn_kernel,
        grid=(B, nblk),
        in_specs=[pl.BlockSpec((1, tq, QW), lambda b, j: (b, j, COL_NQ // QW)),
                  pl.BlockSpec((1, S, QW), im(COL_NK)), pl.BlockSpec((1, S, QW), im(COL_NV)),
                  pl.BlockSpec((1, C, QW), im(COL_NK)), pl.BlockSpec((1, C, QW), im(COL_NV)),
                  pl.BlockSpec((1, N_QHEADS, tq, nk), lambda b, j: (cls(j), 0, 0, 0))],
        out_specs=pl.BlockSpec((1, tq, QW), lambda b, j: (b, j, 0)),
        out_shape=jax.ShapeDtypeStruct((B, S, QW), BF16),
        compiler_params=_cparams(("arbitrary", "arbitrary")),
        name="neighbourhood_attention",
    )(p3, p3, p3, p3c, p3c, table)


def _cattn_kernel(sink_ref, q_ref, k_ref, v_ref, o_ref, *, kv_tiles, use_sink, q_scale):
    C = q_ref.shape[1]
    top = lax.broadcasted_iota(jnp.int32, (2 * C, 1), 0) < C
    for p in range(q_ref.shape[2] // LANES):
        sl = slice(p * LANES, (p + 1) * LANES)
        ksl = sl if kv_tiles > 1 else slice(0, LANES)
        q = q_ref[0, :, sl]
        if q_scale != 1.0:
            q = q * q_scale
        s = _nt_dot(_stack_pair(q), k_ref[0, :, ksl])
        m = jnp.max(s, axis=1, keepdims=True)
        if use_sink:
            sink = jnp.where(top, sink_ref[PAIR_ORDER[2 * p]], sink_ref[PAIR_ORDER[2 * p + 1]])
            m = jnp.maximum(m, sink)
        e = jnp.exp(s - m)
        l = jnp.sum(e, axis=1, keepdims=True)
        if use_sink:
            l = l + jnp.exp(sink - m)
        o = jnp.dot(e.astype(BF16), v_ref[0, :, ksl], preferred_element_type=F32) / l
        o_ref[0, :, sl] = _unstack_pair(o, C).astype(BF16)


def context_attention(sink, q_arr, q_col, k_arr, k_col, v_arr, v_col, kv_tiles, use_sink, q_scale):
    B, C, _ = q_arr.shape
    QW = N_QHEADS * HEAD_DIM
    KW = kv_tiles * LANES
    return pl.pallas_call(
        functools.partial(_cattn_kernel, kv_tiles=kv_tiles, use_sink=use_sink, q_scale=q_scale),
        grid=(B,),
        in_specs=[pl.BlockSpec(memory_space=pltpu.SMEM),
                  pl.BlockSpec((1, C, QW), lambda b: (b, 0, q_col // QW)),
                  pl.BlockSpec((1, C, KW), lambda b: (b, 0, k_col // KW)),
                  pl.BlockSpec((1, C, KW), lambda b: (b, 0, v_col // KW))],
        out_specs=pl.BlockSpec((1, C, QW), lambda b: (b, 0, 0)),
        out_shape=jax.ShapeDtypeStruct((B, C, QW), BF16),
        compiler_params=_cparams(("arbitrary",)),
        name="context_attention",
    )(sink, q_arr, k_arr, v_arr)


def _merge_kernel(h0_ref, h1_ref, mo_ref, ng_ref, ab_ref, ac_ref, ad_ref, g0_ref, g1_ref, g2_ref, g3_ref,
                  wb_ref, wo_ref, x_ref, ga_ref, xo_ref):
    hs = h0_ref[...].astype(F32) + h1_ref[...].astype(F32)
    parts = []
    for h in range(M_HEADS):
        hh = hs[:, h * M_HEAD_DIM:(h + 1) * M_HEAD_DIM]
        parts.append(hh * lax.rsqrt(jnp.mean(hh * hh, axis=1, keepdims=True) + EPS))
    a0 = (jnp.concatenate(parts, axis=1) * ng_ref[...]) * _sigmoid(mo_ref[...].astype(F32))
    branches = (a0.astype(BF16), ab_ref[...], ac_ref[...], ad_ref[...])
    gates = (g0_ref, g1_ref, g2_ref, g3_ref)
    y = None
    for i in range(N_BRANCH):
        t = _sigmoid(gates[i][...].astype(F32)) * jnp.dot(branches[i], wb_ref[i], preferred_element_type=F32)
        y = t if y is None else y + t
    out = jnp.dot(y.astype(BF16), wo_ref[...], preferred_element_type=F32)
    xo_ref[...] = x_ref[...] + ga_ref[0] * out


def merge_branches(h0, h1, p2, norm_g, att_b, att_c, att_d, w_branch, w_out, x, ga, rows_per_mod):
    R, D = x.shape
    tm = min(512, R)
    tpm = rows_per_mod // tm
    row = lambda i: (i, 0)
    fixed = lambda i: (0, 0)
    gate_spec = lambda k: pl.BlockSpec((tm, D), lambda i: (i, COL_GATE // D + k))
    return pl.pallas_call(
        _merge_kernel,
        grid=(R // tm,),
        in_specs=[pl.BlockSpec((tm, M_WIDTH), row), pl.BlockSpec((tm, M_WIDTH), row),
                  pl.BlockSpec((tm, M_WIDTH), lambda i: (i, COL_MO // M_WIDTH)),
                  pl.BlockSpec((1, M_WIDTH), fixed),
                  pl.BlockSpec((tm, BRANCH_WIDTH), row), pl.BlockSpec((tm, BRANCH_WIDTH), row),
                  pl.BlockSpec((tm, BRANCH_WIDTH), row),
                  gate_spec(0), gate_spec(1), gate_spec(2), gate_spec(3),
                  pl.BlockSpec((N_BRANCH, BRANCH_WIDTH, D), lambda i: (0, 0, 0)),
                  pl.BlockSpec((D, D), fixed),
                  pl.BlockSpec((tm, D), row),
                  pl.BlockSpec((1, 1, D), lambda i: (i // tpm, 0, 0))],
        out_specs=pl.BlockSpec((tm, D), row),
        out_shape=jax.ShapeDtypeStruct((R, D), F32),
        compiler_params=_cparams(("arbitrary",)),
        name="merge_branches",
    )(h0, h1, p2, norm_g.reshape(1, -1), att_b, att_c, att_d, p2, p2, p2, p2, w_branch, w_out, x, ga)


def _router_kernel(x_ref, g_ref, sc_ref, sh_ref, wr_ref, rb_ref, h_ref, e_ref, w_ref):
    x = x_ref[...]
    y = x * lax.rsqrt(jnp.mean(x * x, axis=-1, keepdims=True) + EPS)
    h = (y * g_ref[...]) * (1.0 + sc_ref[0]) + sh_ref[0]
    _to_token_tiles(h_ref, h, x.shape[0])
    logits = lax.dot_general(wr_ref[...], h, (((1,), (1,)), ((), ())), precision=HIGHEST,
                             preferred_element_type=F32)
    scores = _sigmoid(logits)
    sel = scores + rb_ref[...]
    E = EXPERTS_PER_GROUP
    tm = x.shape[0]
    sub = lax.broadcasted_iota(jnp.int32, (E, tm), 0)
    best = None
    for g in range(N_GROUPS):
        v = sel[g * E:(g + 1) * E]
        sc = scores[g * E:(g + 1) * E]
        m1 = jnp.max(v, axis=0, keepdims=True)
        i1 = jnp.min(jnp.where(v == m1, sub, E), axis=0, keepdims=True)
        rest = jnp.where(sub == i1, -jnp.inf, v)
        m2 = jnp.max(rest, axis=0, keepdims=True)
        i2 = jnp.min(jnp.where(rest == m2, sub, E), axis=0, keepdims=True)
        s1 = jnp.sum(jnp.where(sub == i1, sc, 0.0), axis=0, keepdims=True)
        s2 = jnp.sum(jnp.where(sub == i2, sc, 0.0), axis=0, keepdims=True)
        cand = (m1 + m2, g * E + i1, g * E + i2, s1, s2)
        if best is None:
            best = cand
        else:
            take = cand[0] > best[0]
            best = tuple(jnp.where(take, cn, bs) for cn, bs in zip(cand, best))
    _, e1, e2, s1, s2 = best
    tot = s1 + s2
    e_ref[0:1, :] = e1
    e_ref[1:2, :] = e2
    w_ref[0:1, :] = s1 / tot
    w_ref[1:2, :] = s2 / tot


def router(x, g, sc, sh, w_router_t, router_bias, rows_per_mod):
    R, D = x.shape
    tm = min(512, R)
    tpm = rows_per_mod // tm
    mod = lambda i: (i // tpm, 0, 0)
    fixed = lambda i: (0, 0)
    return pl.pallas_call(
        _router_kernel,
        grid=(R // tm,),
        in_specs=[pl.BlockSpec((tm, D), lambda i: (i, 0)), pl.BlockSpec((1, D), fixed),
                  pl.BlockSpec((1, 1, D), mod), pl.BlockSpec((1, 1, D), mod),
                  pl.BlockSpec((N_EXPERTS, D), fixed), pl.BlockSpec((N_EXPERTS, 1), fixed)],
        out_specs=[pl.BlockSpec((tm * ROW_TILE, LANES), lambda i: (i, 0)), pl.BlockSpec((2, tm), lambda i: (0, i)),
                   pl.BlockSpec((2, tm), lambda i: (0, i))],
        out_shape=[jax.ShapeDtypeStruct((R * ROW_TILE, LANES), F32), jax.ShapeDtypeStruct((2, R), jnp.int32),
                   jax.ShapeDtypeStruct((2, R), F32)],
        compiler_params=_cparams(("arbitrary",)),
        name="router",
    )(x, g.reshape(1, D), sc, sh, w_router_t, router_bias.reshape(-1, 1))


N_XBUF = 4


def _expert_kernel(be_ref, nu_ref, tok_ref, nxt_ref, nx2_ref, dst_ref, h_hbm, wg_ref, wu_ref, wd_ref, y_hbm,
                   xbuf, ybuf, wgb, wub, wdb, gsem, ssem):
    i = pl.program_id(0)
    n_used = nu_ref[0]
    MB = MOE_BLOCK
    RT = ROW_TILE

    def gather_copy(t8, r, b):
        return pltpu.make_async_copy(h_hbm.at[pl.ds(pl.multiple_of(t8, RT), RT)], xbuf.at[b, pl.ds(r * RT, RT)],
                                     gsem.at[b])

    def scatter_copy(d8, r, b):
        return pltpu.make_async_copy(ybuf.at[b, pl.ds(r * RT, RT)], y_hbm.at[pl.ds(pl.multiple_of(d8, RT), RT)],
                                     ssem.at[b])

    def all_gathers(b):
        return pltpu.make_async_copy(h_hbm.at[pl.ds(0, MB * RT)], xbuf.at[b], gsem.at[b])

    def all_scatters(b):
        return pltpu.make_async_copy(ybuf.at[b], y_hbm.at[pl.ds(0, MB * RT)], ssem.at[b])

    def start_gather(idx_ref, b):
        for r in range(MB):
            gather_copy(idx_ref[0, 0, r], r, b).start(priority=r % 2)

    @pl.when(i == 0)
    def _():
        n_real = y_hbm.shape[0] - 2 * MB * RT
        ybuf[...] = jnp.zeros_like(ybuf)
        for b in range(2):
            fill = pltpu.make_async_copy(ybuf.at[b], y_hbm.at[pl.ds(n_real + b * MB * RT, MB * RT)], ssem.at[b])
            fill.start()
            fill.wait()

    @pl.when(jnp.logical_and(i == 0, n_used > 0))
    def _():
        start_gather(tok_ref, 0)
        start_gather(nxt_ref, 1)

    for k in range(N_XBUF):
        @pl.when(i % N_XBUF == k)
        def _(k=k):
            xb, yb = k, k % 2

            @pl.when(jnp.logical_and(jnp.logical_and(i >= n_used, i < n_used + 2), n_used > 0))
            def _():
                all_gathers(xb).wait()

            @pl.when(jnp.logical_and(i >= 2, i - 2 < n_used))
            def _():
                all_scatters(yb).wait()

            @pl.when(i < n_used)
            def _():
                prev = be_ref[jnp.maximum(i - 1, 0)]

                @pl.when(jnp.logical_or(i == 0, be_ref[i] != prev))
                def _():
                    wgb[...] = wg_ref[0, 0].astype(BF16)
                    wub[...] = wu_ref[0, 0].astype(BF16)
                    wdb[...] = wd_ref[0, 0].astype(BF16)

                all_gathers(xb).wait()
                x = _from_token_tiles(xbuf.at[xb], MB).astype(BF16)
                g = jnp.dot(x, wgb[...], preferred_element_type=F32)
                u = jnp.dot(x, wub[...], preferred_element_type=F32)
                hmid = ((g * _sigmoid(g)) * u).astype(BF16)
                start_gather(nx2_ref, (k + 2) % N_XBUF)
                _to_token_tiles(ybuf.at[yb], jnp.dot(hmid, wdb[...], preferred_element_type=F32), MB)
                for r in range(MB):
                    scatter_copy(dst_ref[0, 0, r], r, yb).start(priority=r % 2)


def expert_ffn(blk_e, n_used, slot_tok, slot_dst, h2, e_gate, e_up, e_down, layer, n_out_rows):
    nblk = slot_tok.shape[0]
    D = D_MODEL
    last = nblk - 1
    MB = MOE_BLOCK
    RT = ROW_TILE
    wmap = lambda i, be, nu: (layer, be[jnp.minimum(i, last)], 0, 0)
    smem_blk = lambda off: pl.BlockSpec((1, 1, MB), lambda i, be, nu: (jnp.minimum(i + off, last), 0, 0),
                                        memory_space=pltpu.SMEM)
    grid_spec = pltpu.PrefetchScalarGridSpec(
        num_scalar_prefetch=2,
        grid=(nblk + 2,),
        in_specs=[smem_blk(0), smem_blk(1), smem_blk(2), smem_blk(0),
                  pl.BlockSpec(memory_space=pl.ANY),
                  pl.BlockSpec((1, 1, D, D_EXPERT), wmap), pl.BlockSpec((1, 1, D, D_EXPERT), wmap),
                  pl.BlockSpec((1, 1, D_EXPERT, D), wmap)],
        out_specs=pl.BlockSpec(memory_space=pl.ANY),
        scratch_shapes=[pltpu.VMEM((N_XBUF, MB * RT, LANES), F32), pltpu.VMEM((2, MB * RT, LANES), F32),
                        pltpu.VMEM((D, D_EXPERT), BF16), pltpu.VMEM((D, D_EXPERT), BF16),
                        pltpu.VMEM((D_EXPERT, D), BF16),
                        pltpu.SemaphoreType.DMA((N_XBUF,)), pltpu.SemaphoreType.DMA((2,))],
    )
    return pl.pallas_call(
        _expert_kernel,
        grid_spec=grid_spec,
        out_shape=jax.ShapeDtypeStruct((n_out_rows * RT, LANES), F32),
        compiler_params=_cparams(("arbitrary",)),
        name="expert_ffn",
    )(blk_e, n_used, slot_tok, slot_tok, slot_tok, slot_dst, h2, e_gate, e_up, e_down)


def moe_ffn(h2, expert, e_gate, e_up, e_down, layer):
    T = expert.shape[1]
    MB = MOE_BLOCK
    n_assign = 2 * T
    e_flat = expert.reshape(-1)
    a_bits = (n_assign - 1).bit_length()
    assert N_EXPERTS << a_bits < 2 ** 31
    keys = e_flat.astype(jnp.int32) * (1 << a_bits) + jnp.arange(n_assign, dtype=jnp.int32)
    order = jnp.sort(keys) & ((1 << a_bits) - 1)
    eids = jnp.arange(N_EXPERTS, dtype=jnp.int32)
    counts = jnp.sum((e_flat[:, None] == eids[None, :]).astype(jnp.int32), axis=0)
    starts = jnp.cumsum(counts) - counts
    padded = ((counts + MB - 1) // MB) * MB
    pends = jnp.cumsum(padded)
    pstarts = pends - padded
    nblk = -(-(n_assign + N_EXPERTS * (MB - 1)) // MB)
    cap = nblk * MB
    slot = jnp.arange(cap, dtype=jnp.int32)
    slot_e = jnp.sum((slot[:, None] >= pends[None, :]).astype(jnp.int32), axis=1)
    onehot = slot_e[:, None] == eids[None, :]
    pick = lambda v: jnp.sum(jnp.where(onehot, v[None, :], 0), axis=1)
    local = slot - pick(pstarts)
    real = jnp.logical_and(slot_e < N_EXPERTS, local < pick(counts))
    slot_assign = order[jnp.clip(pick(starts) + local, 0, n_assign - 1)]
    dump = n_assign + ((slot // MB) % 2) * MB + slot % MB
    slot_dst = (jnp.where(real, slot_assign, dump) * ROW_TILE).reshape(nblk, 1, MB)
    slot_tok = (jnp.where(real, slot_assign % T, 0) * ROW_TILE).reshape(nblk, 1, MB)
    blk_e = jnp.minimum(slot_e.reshape(nblk, MB)[:, 0], N_EXPERTS - 1)
    n_used = (pends[-1] // MB).astype(jnp.int32).reshape(1)
    return expert_ffn(blk_e, n_used, slot_tok, slot_dst, h2, e_gate, e_up, e_down, layer, n_assign + 2 * MB)


def _final_kernel(x_ref, y0_ref, y1_ref, wt_ref, ga_ref, g_ref, o_ref):
    x = x_ref[...] + ga_ref[0] * _moe_combine(y0_ref, y1_ref, wt_ref)
    o_ref[...] = (x * lax.rsqrt(jnp.mean(x * x, axis=-1, keepdims=True) + EPS)) * g_ref[...]


def final_norm(x, moe, ga, g, rows_per_mod):
    R, D = x.shape
    tm = min(1024, R)
    tpm = rows_per_mod // tm
    row = lambda i: (i, 0)
    moe_specs, moe_args = _moe_specs(moe, tm, 1)
    return pl.pallas_call(
        _final_kernel,
        grid=(R // tm,),
        in_specs=[pl.BlockSpec((tm, D), row)] + moe_specs
        + [pl.BlockSpec((1, 1, D), lambda i: (i // tpm, 0, 0)), pl.BlockSpec((1, D), lambda i: (0, 0))],
        out_specs=pl.BlockSpec((tm, D), row),
        out_shape=jax.ShapeDtypeStruct((R, D), F32),
        compiler_params=_cparams(("arbitrary",)),
        name="final_norm",
    )(x, *moe_args, ga, g.reshape(1, D))


_DEINTERLEAVE = np.concatenate([np.arange(0, HEAD_DIM, 2), np.arange(1, HEAD_DIM, 2)])


def _head_cols(n_heads, order, perm):
    return np.concatenate([h * HEAD_DIM + perm for h in order])


def _reorder_w_in(w_in):
    widths = (512, 512, 512, 512, 16, 512, 128, 128, 512, 128, 128, 512, 512, 512, 4096)
    offs = np.concatenate([[0], np.cumsum(widths)])
    (mq, mk, mv, mo, mif, gq, gk, gv, wq, wk, wv, nq, nk, nv, gate) = [
        w_in[:, offs[i]:offs[i + 1]] for i in range(len(widths))]
    D = w_in.shape[0]
    half = HEAD_DIM // 2

    def deinterleave(a, n_heads):
        return a.reshape(D, n_heads, half, 2).transpose(0, 1, 3, 2).reshape(D, n_heads * HEAD_DIM)

    def pair_heads(a):
        return a.reshape(D, 2, N_QHEADS // 2, HEAD_DIM).transpose(0, 2, 1, 3).reshape(D, N_QHEADS * HEAD_DIM)

    w = jnp.concatenate([mq, mk, mv, mo, pair_heads(deinterleave(gq, N_QHEADS)), pair_heads(deinterleave(wq, N_QHEADS)),
                         nq, nk, nv, deinterleave(gk, 2), gv, deinterleave(wk, 2), wv, gate], axis=1).astype(BF16)
    wif = jnp.pad(mif, ((0, 0), (0, LANES - mif.shape[1]))).astype(BF16)
    return w, wif


def _rope_tables(S):
    t = np.arange(S)
    n_freq = HEAD_DIM // 4
    inv_freq = jnp.asarray(ROPE_THETA, F32) ** (-jnp.arange(n_freq, dtype=F32) / n_freq)
    row = jnp.asarray(t // GRID_W, F32)
    col = jnp.asarray(t % GRID_W, F32)
    ang = jnp.concatenate([row[:, None] * inv_freq, col[:, None] * inv_freq], axis=-1)
    cos, sin = jnp.cos(ang), jnp.sin(ang)
    cos_t = jnp.tile(jnp.concatenate([cos, cos], axis=1), (1, 2))
    sin_t = jnp.tile(jnp.concatenate([-sin, sin], axis=1), (1, 2))
    return cos_t, sin_t


def kernel(x, c, ctx, c_ctx, w_mod, b_mod, norm1_g, norm2_g, w_in, m_conv_w, m_conv_b, m_gate_b, m_norm_g,
           g_qnorm, g_knorm, w_sink, n_rel_bias, w_branch, w_out, w_router, router_bias, e_gate, e_up, e_down,
           final_g):
    B, S, D = x.shape
    C = ctx.shape[1]
    L = w_mod.shape[0]
    R, RC = B * S, B * C
    rows = S // GRID_W

    c_all = jnp.zeros((16, D), F32).at[:B].set(c).at[B].set(c_ctx)
    mod = modulation(c_all, w_mod, b_mod).reshape(L, 16, 6, D)
    cos_t, sin_t = _rope_tables(S)
    w_router_t = w_router.T

    xl = x.reshape(R, D)
    xc = ctx.reshape(RC, D)
    moe_lat = moe_ctx = ga2_lat = ga2_ctx = None
    for l in range(L):
        need_ctx = l < L - 1
        lat = [mod[l, :B, i].reshape(B, 1, D) for i in range(6)]
        cx = [mod[l, B:B + 1, i].reshape(1, 1, D) for i in range(6)]
        w, wif = _reorder_w_in(w_in[l])
        gb = jnp.pad(m_gate_b[l].reshape(1, -1), ((0, 0), (0, LANES - 16)))
        gq = jnp.tile(g_qnorm[l][_DEINTERLEAVE], 2).reshape(1, LANES)
        gk = jnp.tile(g_knorm[l][_DEINTERLEAVE], 2).reshape(1, LANES)
        pair_rows = lambda a: a.reshape(2, N_QHEADS // 2, HEAD_DIM, D).transpose(1, 0, 2, 3).reshape(BRANCH_WIDTH, D)
        wb = jnp.stack([w_branch[l, 0], pair_rows(w_branch[l, 1]), pair_rows(w_branch[l, 2]),
                        w_branch[l, 3]]).astype(BF16)
        wo = w_out[l].astype(BF16)

        if l == 0:
            p_lat, g_lat = in_projection(xl, norm1_g[l], lat[1], lat[0], w, wif, gb, S)
            p_ctx, g_ctx = in_projection(xc, norm1_g[l], cx[1], cx[0], w, wif, gb, RC)
        else:
            xl, p_lat, g_lat = in_projection(xl, norm1_g[l], lat[1], lat[0], w, wif, gb, S, moe=moe_lat, ga=ga2_lat)
            xc, p_ctx, g_ctx = in_projection(xc, norm1_g[l], cx[1], cx[0], w, wif, gb, RC, moe=moe_ctx, ga=ga2_ctx)
        p3 = p_lat.reshape(B, S, -1)
        p3c = p_ctx.reshape(B, C, -1)

        qk_c = mlstm_conv(p3c, m_conv_w[l], m_conv_b[l])
        qk_l = mlstm_conv(p3, m_conv_w[l], m_conv_b[l])
        c0 = jnp.zeros((B, 2 * M_HEADS, MV_ROWS, M_HEAD_DIM), F32)
        m0 = jnp.full((B, 2 * M_HEADS, LANES), M_INIT, F32)
        hc0, hc1, c1, m1 = mlstm_scan(qk_c, mlstm_vt(p3c), g_ctx.reshape(B, C, LANES), c0, m0)
        hl0, hl1, _, _ = mlstm_scan(qk_l, mlstm_vt(p3), g_lat.reshape(B, S, LANES), c1, m1)

        scale = HEAD_DIM ** -0.5
        gq_l, gk_l, vt_l = qk_prep(p3, COL_GQ, COL_GK, gq, gk, cos_t, sin_t, True, True, scale * LOG2E, col_v=COL_GV)
        gq_c, gk_c, vt_c = qk_prep(p3c, COL_GQ, COL_GK, gq, gk, cos_t[:C], sin_t[:C], True, False, scale, col_v=COL_GV)
        att_b = global_attention(gq_l, gk_l, vt_l, gk_c, vt_c)

        wq_l, wk_l = qk_prep(p3, COL_WQ, COL_WK, gq, gk, cos_t, sin_t, False, True, scale)
        att_c = window_attention(w_sink[l], wq_l, wk_l, p3, p3c)

        table = neighbourhood_bias_table(n_rel_bias[l], rows)
        att_d = neighbourhood_attention(p3, p3c, table)

        xl = merge_branches(hl0.reshape(R, -1), hl1.reshape(R, -1), p_lat, m_norm_g[l], att_b.reshape(R, -1),
                            att_c.reshape(R, -1), att_d.reshape(R, -1), wb, wo, xl, lat[2], S)
        h2, ex, wt = router(xl, norm2_g[l], lat[4], lat[3], w_router_t, router_bias, S)
        if need_ctx:
            cb = context_attention(w_sink[l], gq_c, 0, gk_c, 0, p3c, COL_GV, 1, False, 1.0)
            cc = context_attention(w_sink[l], p3c, COL_WQ, p3c, COL_WK, p3c, COL_WV, 1, True, scale)
            cd = context_attention(w_sink[l], p3c, COL_NQ, p3c, COL_NK, p3c, COL_NV, 4, False, scale)
            xc = merge_branches(hc0.reshape(RC, -1), hc1.reshape(RC, -1), p_ctx, m_norm_g[l], cb.reshape(RC, -1),
                                cc.reshape(RC, -1), cd.reshape(RC, -1), wb, wo, xc, cx[2], RC)
            h2c, exc, wtc = router(xc, norm2_g[l], cx[4], cx[3], w_router_t, router_bias, RC)
            h2 = jnp.concatenate([h2, h2c], axis=0)
            ex = jnp.concatenate([ex, exc], axis=1)
            wt = jnp.concatenate([wt, wtc], axis=1)
        y = moe_ffn(h2, ex, e_gate, e_up, e_down, l)
        wt_t = wt.T
        moe_lat, ga2_lat = (y, wt_t, 0), lat[5]
        if need_ctx:
            moe_ctx, ga2_ctx = (y, wt_t, R), cx[5]
    out = final_norm(xl, moe_lat, ga2_lat, final_g, S)
    return out.reshape(B, S, D)
```
